```python
import math
import jax, jax.numpy as jnp
from jax import lax
import numpy as np

D_MODEL = 2048
BATCH = 4
SEQ = 8192
DEPTH = 2

N_A = DEPTH // 2
N_B = DEPTH - N_A

POOL_WINDOWS = (2, 4, 8, 16)
N_POOL_GROUPS = len(POOL_WINDOWS)
POOL_GROUP = D_MODEL // N_POOL_GROUPS

N_HEADS = 16
QK_NOPE_DIM = 128
QK_ROPE_DIM = 64
QK_DIM = QK_NOPE_DIM + QK_ROPE_DIM
V_DIM = 128
Q_LORA_RANK = 512
KV_LORA_RANK = 512
ROPE_BASE = 10000.0
ATTN_SCALE = 1.0 / math.sqrt(QK_DIM)
Q_BLOCK = 128

FFN_MULT = 256
D_FF = ((8 * D_MODEL + 3 * FFN_MULT - 1) // (3 * FFN_MULT)) * FFN_MULT

NORM_EPS = 1e-6
POS_OFFSET_MAX = 4096

kernel_name = "yoco_pool_mla_hybrid"


def rms_norm(x, g):
    xf = x.astype(jnp.float32)
    y = xf * lax.rsqrt(jnp.mean(xf * xf, axis=-1, keepdims=True) + NORM_EPS)
    return (y * g.astype(jnp.float32)).astype(x.dtype)


def swiglu(h, w_gate, w_up, w_down):
    return (jax.nn.silu(h @ w_gate) * (h @ w_up)) @ w_down


def rope_tables(positions):
    half = QK_ROPE_DIM // 2
    inv_freq = ROPE_BASE ** (-jnp.arange(half, dtype=jnp.float32) / half)
    ang = positions.astype(jnp.float32)[..., None] * inv_freq
    return jnp.cos(ang), jnp.sin(ang)


def apply_rope(x, cos, sin):
    half = QK_ROPE_DIM // 2
    xf = x.astype(jnp.float32)
    x1, x2 = xf[..., :half], xf[..., half:]
    out = jnp.concatenate([x1 * cos - x2 * sin, x2 * cos + x1 * sin], axis=-1)
    return out.astype(x.dtype)


def multiscale_pool_mixer(h, pool_w, pool_scale):
    S = h.shape[1]
    hf = h.astype(jnp.float32)
    cs = jnp.cumsum(hf, axis=1)
    t = jnp.arange(S)
    outs = []
    for g, w in enumerate(POOL_WINDOWS):
        sl = slice(g * POOL_GROUP, (g + 1) * POOL_GROUP)
        c = cs[..., sl]
        c_prev = jnp.pad(c, ((0, 0), (w, 0), (0, 0)))[:, :S]
        count = jnp.minimum(t + 1, w).astype(jnp.float32)[None, :, None]
        diff = (c - c_prev) / count - hf[..., sl]
        outs.append(jnp.einsum('bsc,cd->bsd', diff.astype(h.dtype), pool_w[g]))
    return jnp.concatenate(outs, axis=-1) * pool_scale


def shared_mla_kv(h, w_kv_a, kv_latent_norm, w_kv_b, cos, sin):
    B, S, _ = h.shape
    kv = h @ w_kv_a
    c_kv = rms_norm(kv[..., :KV_LORA_RANK], kv_latent_norm)
    k_pe = apply_rope(kv[..., KV_LORA_RANK:], cos, sin)
    kv_up = (c_kv @ w_kv_b).reshape(B, S, N_HEADS, QK_NOPE_DIM + V_DIM)
    return kv_up[..., :QK_NOPE_DIM], k_pe, kv_up[..., QK_NOPE_DIM:]


def causal_mla_attention(q_nope, q_pe, k_nope, k_pe, v):
    S = q_nope.shape[1]
    outs = []
    for i in range(S // Q_BLOCK):
        s0, e = i * Q_BLOCK, (i + 1) * Q_BLOCK
        sc = (jnp.einsum('bqhd,bkhd->bhqk', q_nope[:, s0:e], k_nope[:, :e])
              + jnp.einsum('bqhr,bkr->bhqk', q_pe[:, s0:e], k_pe[:, :e]))
        sc = sc.astype(jnp.float32) * ATTN_SCALE
        mask = jnp.arange(e)[None, :] <= (s0 + jnp.arange(Q_BLOCK))[:, None]
        sc = jnp.where(mask, sc, -jnp.inf)
        p = jax.nn.softmax(sc, axis=-1).astype(v.dtype)
        outs.append(jnp.einsum('bhqk,bkhd->bqhd', p, v[:, :e]))
    return jnp.concatenate(outs, axis=1)


def mla_layer_mixer(h, w_q_a, q_latent_norm, w_q_b, w_o, k_nope, k_pe, v, cos, sin):
    B, S, _ = h.shape
    q = (rms_norm(h @ w_q_a, q_latent_norm) @ w_q_b).reshape(B, S, N_HEADS, QK_DIM)
    q_nope = q[..., :QK_NOPE_DIM]
    q_pe = apply_rope(q[..., QK_NOPE_DIM:], cos[:, :, None, :], sin[:, :, None, :])
    o = causal_mla_attention(q_nope, q_pe, k_nope, k_pe, v)
    return o.reshape(B, S, N_HEADS * V_DIM) @ w_o


def setup_inputs(seed: int = 0) -> dict:
    key = jax.random.key(seed)
    ks = jax.random.split(key, 24)
    f32 = jnp.float32

    def dense(k, shape, fan_in):
        return jax.random.normal(k, shape, f32) * (fan_in ** -0.5)

    def gain(k, shape):
        return 1.0 + 0.05 * jax.random.normal(k, shape, f32)

    x = jax.random.normal(ks[0], (BATCH, SEQ, D_MODEL), f32)
    positions = (jax.random.randint(ks[1], (BATCH, 1), 0, POS_OFFSET_MAX, dtype=jnp.int32)
                 + jnp.arange(SEQ, dtype=jnp.int32)[None, :])
    return {
        "x": x,
        "positions": positions,
        "pool_norm": gain(ks[2], (N_A, D_MODEL)),
        "pool_w": dense(ks[3], (N_A, N_POOL_GROUPS, POOL_GROUP, POOL_GROUP), POOL_GROUP),
        "pool_scale": gain(ks[4], (N_A, D_MODEL)),
        "kv_in_norm": gain(ks[5], (D_MODEL,)),
        "w_kv_a": dense(ks[6], (D_MODEL, KV_LORA_RANK + QK_ROPE_DIM), D_MODEL),
        "kv_latent_norm": gain(ks[7], (KV_LORA_RANK,)),
        "w_kv_b": dense(ks[8], (KV_LORA_RANK, N_HEADS * (QK_NOPE_DIM + V_DIM)), KV_LORA_RANK),
        "attn_norm": gain(ks[9], (N_B, D_MODEL)),
        "w_q_a": dense(ks[10], (N_B, D_MODEL, Q_LORA_RANK), D_MODEL),
        "q_latent_norm": gain(ks[11], (N_B, Q_LORA_RANK)),
        "w_q_b": dense(ks[12], (N_B, Q_LORA_RANK, N_HEADS * QK_DIM), Q_LORA_RANK),
        "w_o": dense(ks[13], (N_B, N_HEADS * V_DIM, D_MODEL), N_HEADS * V_DIM),
        "ffn_norm": gain(ks[14], (DEPTH, D_MODEL)),
        "w_gate": dense(ks[15], (DEPTH, D_MODEL, D_FF), D_MODEL),
        "w_up": dense(ks[16], (DEPTH, D_MODEL, D_FF), D_MODEL),
        "w_down": dense(ks[17], (DEPTH, D_FF, D_MODEL), D_FF),
        "final_norm": gain(ks[18], (D_MODEL,)),
    }


def reference(x, positions, pool_norm, pool_w, pool_scale, kv_in_norm, w_kv_a,
              kv_latent_norm, w_kv_b, attn_norm, w_q_a, q_latent_norm, w_q_b, w_o,
              ffn_norm, w_gate, w_up, w_down, final_norm):
    cos, sin = rope_tables(positions)
    k_nope = k_pe = v = None
    for l in range(DEPTH):
        if l == N_A:
            k_nope, k_pe, v = shared_mla_kv(rms_norm(x, kv_in_norm), w_kv_a,
                                            kv_latent_norm, w_kv_b, cos, sin)
        if l < N_A:
            x = x + multiscale_pool_mixer(rms_norm(x, pool_norm[l]), pool_w[l], pool_scale[l])
        else:
            b = l - N_A
            x = x + mla_layer_mixer(rms_norm(x, attn_norm[b]), w_q_a[b], q_latent_norm[b],
                                    w_q_b[b], w_o[b], k_nope, k_pe, v, cos, sin)
        x = x + swiglu(rms_norm(x, ffn_norm[l]), w_gate[l], w_up[l], w_down[l])
    return rms_norm(x, final_norm)
```

```python
import functools
import math

import jax
import jax.numpy as jnp
from jax import lax
from jax.experimental import pallas as pl
from jax.experimental.pallas import tpu as pltpu

N_HEADS = 16
QK_NOPE_DIM = 128
QK_ROPE_DIM = 64
QK_DIM = QK_NOPE_DIM + QK_ROPE_DIM
V_DIM = 128
KV_LORA_RANK = 512
POOL_WINDOWS = (2, 4, 8, 16)
ROPE_BASE = 10000.0
NORM_EPS = 1e-6

LANE = 128
HALO = max(POOL_WINDOWS)
Q_PRESCALE = (1.0 / math.sqrt(QK_DIM)) * math.log2(math.e)

ROW_TILE = 512
FF_TILE = 512
ATTN_TILE = 512
VMEM_LIMIT = 56 * 1024 * 1024

F32 = jnp.float32
BF16 = jnp.bfloat16


def _rms(x, g):
    ms = jnp.mean(x * x, axis=-1, keepdims=True)
    return x * lax.rsqrt(ms + NORM_EPS) * g


def _params(semantics):
    return pltpu.CompilerParams(dimension_semantics=semantics, vmem_limit_bytes=VMEM_LIMIT)


def _pool_kernel(x_ref, halo_ref, g_ref, w_ref, sc_ref, o_ref, hh_ref, *, tiles_per_seq, tm, pg):
    t = pl.program_id(0) % tiles_per_seq
    g = g_ref[...]
    hh_ref[HALO:, :] = _rms(x_ref[...], g)
    hh_ref[:HALO, :] = jnp.where(t == 0, 0.0, _rms(halo_ref[...], g))
    pos = t * tm + lax.broadcasted_iota(jnp.int32, (tm, 1), 0)
    for gi, w in enumerate(POOL_WINDOWS):
        cols = slice(gi * pg, (gi + 1) * pg)
        h = hh_ref[HALO:, cols]
        acc = h
        for j in range(1, w):
            acc = acc + hh_ref[HALO - j:HALO - j + tm, cols]
        count = jnp.minimum(pos + 1, w).astype(F32)
        diff = acc / count - h
        y = jnp.dot(diff.astype(BF16), w_ref[gi], preferred_element_type=F32)
        o_ref[:, cols] = x_ref[:, cols] + y * sc_ref[:, cols]


def _pool_mixer(x, seq, norm, w, scale):
    tokens, d = x.shape
    tm = ROW_TILE
    pg = d // len(POOL_WINDOWS)
    kern = functools.partial(_pool_kernel, tiles_per_seq=seq // tm, tm=tm, pg=pg)
    return pl.pallas_call(
        kern,
        name="pool_mixer",
        grid=(tokens // tm,),
        in_specs=[
            pl.BlockSpec((tm, d), lambda i: (i, 0)),
            pl.BlockSpec((HALO, d), lambda i: (jnp.maximum(i * (tm // HALO) - 1, 0), 0)),
            pl.BlockSpec((1, d), lambda i: (0, 0)),
            pl.BlockSpec(w.shape, lambda i: (0, 0, 0)),
            pl.BlockSpec((1, d), lambda i: (0, 0)),
        ],
        out_specs=pl.BlockSpec((tm, d), lambda i: (i, 0)),
        out_shape=jax.ShapeDtypeStruct((tokens, d), F32),
        scratch_shapes=[pltpu.VMEM((HALO + tm, d), F32)],
        compiler_params=_params(("arbitrary",)),
    )(x, x, norm, w, scale)


def _ffn_kernel(x_ref, g_ref, wg_ref, wu_ref, wd_ref, fg_ref, o_ref, hn_ref, *, final_norm):
    j = pl.program_id(1)

    @pl.when(j == 0)
    def _():
        x = x_ref[...]
        hn_ref[...] = _rms(x, g_ref[...]).astype(BF16)
        o_ref[...] = x

    hn = hn_ref[...]
    gate = jnp.dot(hn, wg_ref[...], preferred_element_type=F32)
    up = jnp.dot(hn, wu_ref[...], preferred_element_type=F32)
    act = gate / (1.0 + jnp.exp(-gate)) * up
    o_ref[...] += jnp.dot(act.astype(BF16), wd_ref[...], preferred_element_type=F32)

    if final_norm:
        @pl.when(j == pl.num_programs(1) - 1)
        def _():
            o_ref[...] = _rms(o_ref[...], fg_ref[...])


def _ffn(x, norm, w_gate, w_up, w_down, final_g, *, final_norm):
    tokens, d = x.shape
    dff = w_gate.shape[1]
    tm, tf = ROW_TILE, FF_TILE
    kern = functools.partial(_ffn_kernel, final_norm=final_norm)
    return pl.pallas_call(
        kern,
        name="ffn_final" if final_norm else "ffn",
        grid=(tokens // tm, dff // tf),
        in_specs=[
            pl.BlockSpec((tm, d), lambda i, j: (i, 0)),
            pl.BlockSpec((1, d), lambda i, j: (0, 0)),
            pl.BlockSpec((d, tf), lambda i, j: (0, j)),
            pl.BlockSpec((d, tf), lambda i, j: (0, j)),
            pl.BlockSpec((tf, d), lambda i, j: (j, 0)),
            pl.BlockSpec((1, d), lambda i, j: (0, 0)),
        ],
        out_specs=pl.BlockSpec((tm, d), lambda i, j: (i, 0)),
        out_shape=jax.ShapeDtypeStruct((tokens, d), F32),
        scratch_shapes=[pltpu.VMEM((tm, d), BF16)],
        compiler_params=_params(("arbitrary", "arbitrary")),
    )(x, norm, w_gate, w_up, w_down, final_g)


def _kv_kernel(x_ref, g_ref, wa_ref, gl_ref, wb_ref, cos_ref, sin_ref, kv_ref, kpe_ref):
    h = _rms(x_ref[...], g_ref[...]).astype(BF16)
    kv = jnp.dot(h, wa_ref[...], preferred_element_type=F32)
    c_kv = _rms(kv[:, :KV_LORA_RANK], gl_ref[...]).astype(BF16)
    pe = kv[:, KV_LORA_RANK:KV_LORA_RANK + LANE]
    pe_swapped = kv[:, KV_LORA_RANK + LANE:]
    rot = pe * cos_ref[...] + pe_swapped * sin_ref[...]
    kpe_ref[:, :LANE] = rot.astype(BF16)
    kpe_ref[:, LANE:] = pltpu.roll(rot, LANE // 2, axis=1).astype(BF16)
    kv_ref[...] = jnp.dot(c_kv, wb_ref[...], preferred_element_type=F32).astype(BF16)


def _kv_proj(x, norm, wa_ext, latent_norm, wb, cos, sin):
    tokens, d = x.shape
    tm = ROW_TILE
    n_up = wb.shape[1]
    return pl.pallas_call(
        _kv_kernel,
        name="kv_proj",
        grid=(tokens // tm,),
        in_specs=[
            pl.BlockSpec((tm, d), lambda i: (i, 0)),
            pl.BlockSpec((1, d), lambda i: (0, 0)),
            pl.BlockSpec(wa_ext.shape, lambda i: (0, 0)),
            pl.BlockSpec((1, KV_LORA_RANK), lambda i: (0, 0)),
            pl.BlockSpec(wb.shape, lambda i: (0, 0)),
            pl.BlockSpec((tm, LANE), lambda i: (i, 0)),
            pl.BlockSpec((tm, LANE), lambda i: (i, 0)),
        ],
        out_specs=[
            pl.BlockSpec((tm, n_up), lambda i: (i, 0)),
            pl.BlockSpec((tm, 2 * LANE), lambda i: (i, 0)),
        ],
        out_shape=[
            jax.ShapeDtypeStruct((tokens, n_up), BF16),
            jax.ShapeDtypeStruct((tokens, 2 * LANE), BF16),
        ],
        compiler_params=_params(("arbitrary",)),
    )(x, norm, wa_ext, latent_norm, wb, cos, sin)


def _q_kernel(x_ref, g_ref, wa_ref, gl_ref, wb_ref, cos_ref, sin_ref, qn_ref, qp_ref, *, n_nope, n_pe):
    h = _rms(x_ref[...], g_ref[...]).astype(BF16)
    cq = jnp.dot(h, wa_ref[...], preferred_element_type=F32)
    cqn = _rms(cq, gl_ref[...]).astype(BF16)
    q = jnp.dot(cqn, wb_ref[...], preferred_element_type=F32)
    qn_ref[...] = (q[:, :n_nope] * Q_PRESCALE).astype(BF16)
    cos = cos_ref[...] * Q_PRESCALE
    sin = sin_ref[...] * Q_PRESCALE
    for p in range(n_pe // LANE):
        pe = q[:, n_nope + p * LANE:n_nope + (p + 1) * LANE]
        sw = q[:, n_nope + n_pe + p * LANE:n_nope + n_pe + (p + 1) * LANE]
        qp_ref[:, p * LANE:(p + 1) * LANE] = (pe * cos + sw * sin).astype(BF16)


def _q_proj(x, norm, wa, latent_norm, wb_ext, cos, sin):
    tokens, d = x.shape
    tm = ROW_TILE
    n_nope = N_HEADS * QK_NOPE_DIM
    n_pe = N_HEADS * QK_ROPE_DIM
    kern = functools.partial(_q_kernel, n_nope=n_nope, n_pe=n_pe)
    return pl.pallas_call(
        kern,
        name="q_proj",
        grid=(tokens // tm,),
        in_specs=[
            pl.BlockSpec((tm, d), lambda i: (i, 0)),
            pl.BlockSpec((1, d), lambda i: (0, 0)),
            pl.BlockSpec(wa.shape, lambda i: (0, 0)),
            pl.BlockSpec((1, wa.shape[1]), lambda i: (0, 0)),
            pl.BlockSpec(wb_ext.shape, lambda i: (0, 0)),
            pl.BlockSpec((tm, LANE), lambda i: (i, 0)),
            pl.BlockSpec((tm, LANE), lambda i: (i, 0)),
        ],
        out_specs=[
            pl.BlockSpec((tm, n_nope), lambda i: (i, 0)),
            pl.BlockSpec((tm, n_pe), lambda i: (i, 0)),
        ],
        out_shape=[
            jax.ShapeDtypeStruct((tokens, n_nope), BF16),
            jax.ShapeDtypeStruct((tokens, n_pe), BF16),
        ],
        compiler_params=_params(("arbitrary",)),
    )(x, norm, wa, latent_norm, wb_ext, cos, sin)


def _attn_kernel(qn_ref, qp_ref, kn_ref, kp_ref, v_ref, o_ref, kcat_ref, *, blk):
    qi = pl.program_id(2)

    @pl.when(qi == 0)
    def _():
        kcat_ref[:, :LANE] = kn_ref[...]
        kcat_ref[:, LANE:] = kp_ref[...]

    q = jnp.concatenate([qn_ref[...], qp_ref[...]], axis=1)

    def step(j, carry, masked):
        m, l, acc = carry
        off = pl.multiple_of(j * blk, blk)
        k = kcat_ref[pl.ds(off, blk), :]
        v = v_ref[pl.ds(off, blk), :]
        s = lax.dot_general(q, k, (((1,), (1,)), ((), ())), preferred_element_type=F32)
        if masked:
            row = lax.broadcasted_iota(jnp.int32, (blk, blk), 0)
            col = lax.broadcasted_iota(jnp.int32, (blk, blk), 1)
            s = jnp.where(col <= row, s, -jnp.inf)
        m_new = jnp.maximum(m, jnp.max(s, axis=1, keepdims=True))
        alpha = jnp.exp2(m - m_new)
        p = jnp.exp2(s - m_new)
        l = alpha * l + jnp.sum(p, axis=1, keepdims=True)
        acc = alpha * acc + jnp.dot(p.astype(BF16), v, preferred_element_type=F32)
        return m_new, l, acc

    init = (jnp.full((blk, 1), -jnp.inf, F32), jnp.zeros((blk, 1), F32), jnp.zeros((blk, V_DIM), F32))
    carry = lax.fori_loop(0, qi, functools.partial(step, masked=False), init)
    _, l, acc = step(qi, carry, masked=True)
    o_ref[...] = (acc / l).astype(BF16)


def _attention(qn, qp, kv_up, kpe, batch, seq):
    tokens = qn.shape[0]
    blk = ATTN_TILE
    nq = seq // blk
    kern = functools.partial(_attn_kernel, blk=blk)
    return pl.pallas_call(
        kern,
        name="attention",
        grid=(batch, N_HEADS, nq),
        in_specs=[
            pl.BlockSpec((blk, LANE), lambda b, h, i: (b * nq + i, h)),
            pl.BlockSpec((blk, LANE), lambda b, h, i: (b * nq + i, h // 2)),
            pl.BlockSpec((seq, LANE), lambda b, h, i: (b, 2 * h)),
            pl.BlockSpec((seq, LANE), lambda b, h, i: (b, h % 2)),
            pl.BlockSpec((seq, LANE), lambda b, h, i: (b, 2 * h + 1)),
        ],
        out_specs=pl.BlockSpec((blk, LANE), lambda b, h, i: (b * nq + i, h)),
        out_shape=jax.ShapeDtypeStruct((tokens, N_HEADS * V_DIM), BF16),
        scratch_shapes=[pltpu.VMEM((seq, 2 * LANE), BF16)],
        compiler_params=_params(("arbitrary", "arbitrary", "arbitrary")),
    )(qn, qp, kv_up, kpe, kv_up)


def _out_proj_kernel(x_ref, a_ref, w_ref, o_ref):
    o_ref[...] = x_ref[...] + jnp.dot(a_ref[...], w_ref[...], preferred_element_type=F32)


def _out_proj(x, attn, w_o):
    tokens, d = x.shape
    tm = ROW_TILE
    return pl.pallas_call(
        _out_proj_kernel,
        name="out_proj",
        grid=(tokens // tm,),
        in_specs=[
            pl.BlockSpec((tm, d), lambda i: (i, 0)),
            pl.BlockSpec((tm, attn.shape[1]), lambda i: (i, 0)),
            pl.BlockSpec(w_o.shape, lambda i: (0, 0)),
        ],
        out_specs=pl.BlockSpec((tm, d), lambda i: (i, 0)),
        out_shape=jax.ShapeDtypeStruct((tokens, d), F32),
        compiler_params=_params(("arbitrary",)),
    )(x, attn, w_o)


def _swap_halves(w):
    half = w.shape[-1] // 2
    return jnp.concatenate([w[..., half:], w[..., :half]], axis=-1)


def _rope_tables(positions):
    half = QK_ROPE_DIM // 2
    inv_freq = ROPE_BASE ** (-jnp.arange(half, dtype=F32) / half)
    ang = positions.astype(F32).reshape(-1, 1) * inv_freq
    cos, sin = jnp.cos(ang), jnp.sin(ang)
    return (jnp.concatenate([cos, cos, cos, cos], axis=-1),
            jnp.concatenate([-sin, sin, -sin, sin], axis=-1))


def kernel(x, positions, pool_norm, pool_w, pool_scale, kv_in_norm, w_kv_a, kv_latent_norm, w_kv_b, attn_norm, w_q_a, q_latent_norm, w_q_b, w_o, ffn_norm, w_gate, w_up, w_down, final_norm):
    batch, seq, d = x.shape
    depth = ffn_norm.shape[0]
    n_pool = pool_norm.shape[0]
    assert seq % ROW_TILE == 0 and seq % ATTN_TILE == 0
    assert w_gate.shape[2] % FF_TILE == 0

    cos, sin = _rope_tables(positions)
    row = lambda v: v.reshape(1, -1)
    xs = x.reshape(batch * seq, d)

    wg, wu, wd = w_gate.astype(BF16), w_up.astype(BF16), w_down.astype(BF16)
    zpad = jnp.zeros((d, LANE - QK_ROPE_DIM), F32)
    w_pe = w_kv_a[:, KV_LORA_RANK:]
    wa_ext = jnp.concatenate([w_kv_a[:, :KV_LORA_RANK], w_pe, zpad, _swap_halves(w_pe), zpad], axis=1).astype(BF16)
    wkb = w_kv_b.astype(BF16)

    kv_up = kpe = None
    for l in range(depth):
        if l == n_pool:
            kv_up, kpe = _kv_proj(xs, row(kv_in_norm), wa_ext, row(kv_latent_norm), wkb, cos, sin)
        if l < n_pool:
            xs = _pool_mixer(xs, seq, row(pool_norm[l]), pool_w[l].astype(BF16), row(pool_scale[l]))
        else:
            b = l - n_pool
            rank = w_q_b.shape[1]
            wqb = w_q_b[b].reshape(rank, N_HEADS, QK_DIM)
            wqb_pe = wqb[:, :, QK_NOPE_DIM:]
            wqb_ext = jnp.concatenate([
                wqb[:, :, :QK_NOPE_DIM].reshape(rank, -1),
                wqb_pe.reshape(rank, -1),
                _swap_halves(wqb_pe).reshape(rank, -1)], axis=1).astype(BF16)
            qn, qp = _q_proj(xs, row(attn_norm[b]), w_q_a[b].astype(BF16), row(q_latent_norm[b]), wqb_ext, cos, sin)
            attn = _attention(qn, qp, kv_up, kpe, batch, seq)
            xs = _out_proj(xs, attn, w_o[b].astype(BF16))
        last = l == depth - 1
        xs = _ffn(xs, row(ffn_norm[l]), wg[l], wu[l], wd[l], row(final_norm), final_norm=last)
    return xs.reshape(batch, seq, d)
```

```python
import functools
import math

import jax
import jax.numpy as jnp
from jax import lax
from jax.experimental import pallas as pl
from jax.experimental.pallas import tpu as pltpu

N_HEADS = 16
QK_NOPE_DIM = 128
QK_ROPE_DIM = 64
QK_DIM = QK_NOPE_DIM + QK_ROPE_DIM
V_DIM = 128
KV_LORA_RANK = 512
POOL_WINDOWS = (2, 4, 8, 16)
ROPE_BASE = 10000.0
NORM_EPS = 1e-6

LANE = 128
HALO = max(POOL_WINDOWS)
Q_PRESCALE = (1.0 / math.sqrt(QK_DIM)) * math.log2(math.e)

ROW_TILE = 512
FF_TILE = 512
ATTN_TILE = 512
VMEM_LIMIT = 56 * 1024 * 1024

F32 = jnp.float32
BF16 = jnp.bfloat16


def _rms(x, g):
    ms = jnp.mean(x * x, axis=-1, keepdims=True)
    return x * lax.rsqrt(ms + NORM_EPS) * g


def _params(semantics):
    return pltpu.CompilerParams(dimension_semantics=semantics, vmem_limit_bytes=VMEM_LIMIT)


def _pool_kernel(x_ref, halo_ref, g_ref, w_ref, sc_ref, o_ref, hh_ref, *, tiles_per_seq, tm, pg):
    t = pl.program_id(0) % tiles_per_seq
    g = g_ref[...]
    hh_ref[HALO:, :] = _rms(x_ref[...], g)
    hh_ref[:HALO, :] = jnp.where(t == 0, 0.0, _rms(halo_ref[...], g))
    pos = t * tm + lax.broadcasted_iota(jnp.int32, (tm, 1), 0)
    for gi, w in enumerate(POOL_WINDOWS):
        cols = slice(gi * pg, (gi + 1) * pg)
        h = hh_ref[HALO:, cols]
        acc = h
        for j in range(1, w):
            acc = acc + hh_ref[HALO - j:HALO - j + tm, cols]
        count = jnp.minimum(pos + 1, w).astype(F32)
        diff = acc / count - h
        y = jnp.dot(diff.astype(BF16), w_ref[gi], preferred_element_type=F32)
        o_ref[:, cols] = x_ref[:, cols] + y * sc_ref[:, cols]


def _pool_mixer(x, seq, norm, w, scale):
    tokens, d = x.shape
    tm = ROW_TILE
    pg = d // len(POOL_WINDOWS)
    kern = functools.partial(_pool_kernel, tiles_per_seq=seq // tm, tm=tm, pg=pg)
    return pl.pallas_call(
        kern,
        name="pool_mixer",
        grid=(tokens // tm,),
        in_specs=[
            pl.BlockSpec((tm, d), lambda i: (i, 0)),
            pl.BlockSpec((HALO, d), lambda i: (jnp.maximum(i * (tm // HALO) - 1, 0), 0)),
            pl.BlockSpec((1, d), lambda i: (0, 0)),
            pl.BlockSpec(w.shape, lambda i: (0, 0, 0)),
            pl.BlockSpec((1, d), lambda i: (0, 0)),
        ],
        out_specs=pl.BlockSpec((tm, d), lambda i: (i, 0)),
        out_shape=jax.ShapeDtypeStruct((tokens, d), F32),
        scratch_shapes=[pltpu.VMEM((HALO + tm, d), F32)],
        compiler_params=_params(("arbitrary",)),
    )(x, x, norm, w, scale)


def _ffn_kernel(x_ref, g_ref, wg_ref, wu_ref, wd_ref, fg_ref, o_ref, hn_ref, *, final_norm):
    j = pl.program_id(1)

    @pl.when(j == 0)
    def _():
        x = x_ref[...]
        hn_ref[...] = _rms(x, g_ref[...]).astype(BF16)
        o_ref[...] = x

    hn = hn_ref[...]
    gate = jnp.dot(hn, wg_ref[...], preferred_element_type=F32)
    up = jnp.dot(hn, wu_ref[...], preferred_element_type=F32)
    act = gate / (1.0 + jnp.exp(-gate)) * up
    o_ref[...] += jnp.dot(act.astype(BF16), wd_ref[...], preferred_element_type=F32)

    if final_norm:
        @pl.when(j == pl.num_programs(1) - 1)
        def _():
            o_ref[...] = _rms(o_ref[...], fg_ref[...])


def _ffn(x, norm, w_gate, w_up, w_down, final_g, *, final_norm):
    tokens, d = x.shape
    dff = w_gate.shape[1]
    tm, tf = ROW_TILE, FF_TILE
    kern = functools.partial(_ffn_kernel, final_norm=final_norm)
    return pl.pallas_call(
        kern,
        name="ffn_final" if final_norm else "ffn",
        grid=(tokens // tm, dff // tf),
        in_specs=[
            pl.BlockSpec((tm, d), lambda i, j: (i, 0)),
            pl.BlockSpec((1, d), lambda i, j: (0, 0)),
            pl.BlockSpec((d, tf), lambda i, j: (0, j)),
            pl.BlockSpec((d, tf), lambda i, j: (0, j)),
            pl.BlockSpec((tf, d), lambda i, j: (j, 0)),
            pl.BlockSpec((1, d), lambda i, j: (0, 0)),
        ],
        out_specs=pl.BlockSpec((tm, d), lambda i, j: (i, 0)),
        out_shape=jax.ShapeDtypeStruct((tokens, d), F32),
        scratch_shapes=[pltpu.VMEM((tm, d), BF16)],
        compiler_params=_params(("arbitrary", "arbitrary")),
    )(x, norm, w_gate, w_up, w_down, final_g)


def _kv_kernel(x_ref, g_ref, wa_ref, gl_ref, wkn_ref, wvt_ref, cos_ref, sin_ref, kn_ref, vt_ref, kpe_ref):
    h = _rms(x_ref[...], g_ref[...]).astype(BF16)
    kv = jnp.dot(h, wa_ref[...], preferred_element_type=F32)
    c_kv = _rms(kv[:, :KV_LORA_RANK], gl_ref[...]).astype(BF16)
    pe = kv[:, KV_LORA_RANK:KV_LORA_RANK + LANE]
    pe_swapped = kv[:, KV_LORA_RANK + LANE:]
    rot = pe * cos_ref[...] + pe_swapped * sin_ref[...]
    kpe_ref[:, :LANE] = rot.astype(BF16)
    kpe_ref[:, LANE:] = pltpu.roll(rot, LANE // 2, axis=1).astype(BF16)
    kn_ref[...] = jnp.dot(c_kv, wkn_ref[...], preferred_element_type=F32).astype(BF16)
    vt = lax.dot_general(wvt_ref[...], c_kv, (((1,), (1,)), ((), ())), preferred_element_type=F32)
    vt_ref[0] = vt.astype(BF16)


def _kv_proj(x, norm, wa_ext, latent_norm, w_kn, w_vt, cos, sin):
    tokens, d = x.shape
    tm = ATTN_TILE
    n_kn, n_v = w_kn.shape[1], w_vt.shape[0]
    return pl.pallas_call(
        _kv_kernel,
        name="kv_proj",
        grid=(tokens // tm,),
        in_specs=[
            pl.BlockSpec((tm, d), lambda i: (i, 0)),
            pl.BlockSpec((1, d), lambda i: (0, 0)),
            pl.BlockSpec(wa_ext.shape, lambda i: (0, 0)),
            pl.BlockSpec((1, KV_LORA_RANK), lambda i: (0, 0)),
            pl.BlockSpec(w_kn.shape, lambda i: (0, 0)),
            pl.BlockSpec(w_vt.shape, lambda i: (0, 0)),
            pl.BlockSpec((tm, LANE), lambda i: (i, 0)),
            pl.BlockSpec((tm, LANE), lambda i: (i, 0)),
        ],
        out_specs=[
            pl.BlockSpec((tm, n_kn), lambda i: (i, 0)),
            pl.BlockSpec((1, n_v, tm), lambda i: (i, 0, 0)),
            pl.BlockSpec((tm, 2 * LANE), lambda i: (i, 0)),
        ],
        out_shape=[
            jax.ShapeDtypeStruct((tokens, n_kn), BF16),
            jax.ShapeDtypeStruct((tokens // tm, n_v, tm), BF16),
            jax.ShapeDtypeStruct((tokens, 2 * LANE), BF16),
        ],
        compiler_params=_params(("arbitrary",)),
    )(x, norm, wa_ext, latent_norm, w_kn, w_vt, cos, sin)


def _q_kernel(x_ref, g_ref, wa_ref, gl_ref, wb_ref, cos_ref, sin_ref, qn_ref, qp_ref, *, n_nope, n_pe):
    h = _rms(x_ref[...], g_ref[...]).astype(BF16)
    cq = jnp.dot(h, wa_ref[...], preferred_element_type=F32)
    cqn = _rms(cq, gl_ref[...]).astype(BF16)
    q = jnp.dot(cqn, wb_ref[...], preferred_element_type=F32)
    qn_ref[...] = (q[:, :n_nope] * Q_PRESCALE).astype(BF16)
    cos = cos_ref[...] * Q_PRESCALE
    sin = sin_ref[...] * Q_PRESCALE
    for p in range(n_pe // LANE):
        pe = q[:, n_nope + p * LANE:n_nope + (p + 1) * LANE]
        sw = q[:, n_nope + n_pe + p * LANE:n_nope + n_pe + (p + 1) * LANE]
        qp_ref[:, p * LANE:(p + 1) * LANE] = (pe * cos + sw * sin).astype(BF16)


def _q_proj(x, norm, wa, latent_norm, wb_ext, cos, sin):
    tokens, d = x.shape
    tm = ROW_TILE
    n_nope = N_HEADS * QK_NOPE_DIM
    n_pe = N_HEADS * QK_ROPE_DIM
    kern = functools.partial(_q_kernel, n_nope=n_nope, n_pe=n_pe)
    return pl.pallas_call(
        kern,
        name="q_proj",
        grid=(tokens // tm,),
        in_specs=[
            pl.BlockSpec((tm, d), lambda i: (i, 0)),
            pl.BlockSpec((1, d), lambda i: (0, 0)),
            pl.BlockSpec(wa.shape, lambda i: (0, 0)),
            pl.BlockSpec((1, wa.shape[1]), lambda i: (0, 0)),
            pl.BlockSpec(wb_ext.shape, lambda i: (0, 0)),
            pl.BlockSpec((tm, LANE), lambda i: (i, 0)),
            pl.BlockSpec((tm, LANE), lambda i: (i, 0)),
        ],
        out_specs=[
            pl.BlockSpec((tm, n_nope), lambda i: (i, 0)),
            pl.BlockSpec((tm, n_pe), lambda i: (i, 0)),
        ],
        out_shape=[
            jax.ShapeDtypeStruct((tokens, n_nope), BF16),
            jax.ShapeDtypeStruct((tokens, n_pe), BF16),
        ],
        compiler_params=_params(("arbitrary",)),
    )(x, norm, wa, latent_norm, wb_ext, cos, sin)


def _attn_kernel(qn_ref, qp_ref, kn_ref, kp_ref, vt_ref, o_ref, st_ref, acc_ref, *, blk):
    qi = pl.program_id(2)
    qp = qp_ref[...]
    qs = [jnp.concatenate([qn_ref[:, h * LANE:(h + 1) * LANE], qp], axis=1) for h in range(2)]
    acc_ref[...] = jnp.zeros(acc_ref.shape, F32)

    def scores(h, j):
        off = pl.multiple_of(j * blk, blk)
        k = jnp.concatenate([kn_ref[pl.ds(off, blk), h * LANE:(h + 1) * LANE],
                             kp_ref[pl.ds(off, blk), h * LANE:(h + 1) * LANE]], axis=1)
        return lax.dot_general(k, qs[h], (((1,), (1,)), ((), ())), preferred_element_type=F32)

    def consume(h, slot, vblk, m, l, cmax):
        m_new = jnp.maximum(m, cmax)
        alpha = jnp.exp2(m - m_new)
        pt = jnp.exp2(st_ref[h, slot] - m_new)
        l = alpha * l + jnp.sum(pt, axis=0, keepdims=True)
        vt = vt_ref[vblk, h * V_DIM:(h + 1) * V_DIM, :]
        acc_ref[h] = alpha * acc_ref[h] + jnp.dot(vt, pt.astype(BF16), preferred_element_type=F32)
        return m_new, l

    def step(t, carry, slot):
        vblk = jnp.where(t == 0, qi, t - 1)
        out = []
        for h in range(2):
            m, l, cmax = carry[h]
            st = scores(h, t)
            st_ref[h, 1 - slot] = st
            m, l = consume(h, slot, vblk, m, l, cmax)
            out.append((m, l, jnp.max(st, axis=0, keepdims=True)))
        return tuple(out)

    key = lax.broadcasted_iota(jnp.int32, (blk, blk), 0)
    qry = lax.broadcasted_iota(jnp.int32, (blk, blk), 1)
    carry = []
    for h in range(2):
        st = jnp.where(key <= qry, scores(h, qi), -jnp.inf)
        st_ref[h, 0] = st
        carry.append((jnp.full((1, blk), -jnp.inf, F32), jnp.zeros((1, blk), F32),
                      jnp.max(st, axis=0, keepdims=True)))

    def pair(u, carry):
        return step(2 * u + 1, step(2 * u, carry, 0), 1)

    carry = lax.fori_loop(0, qi // 2, pair, tuple(carry))

    def finish(carry, slot):
        vblk = jnp.maximum(qi - 1, 0)
        return tuple(consume(h, slot, vblk, *carry[h])[1] for h in range(2))

    ls = lax.cond(qi % 2 == 1,
                  lambda c: finish(step(qi - 1, c, 0), 1),
                  lambda c: finish(c, 0),
                  carry)
    for h in range(2):
        o_ref[:, h * LANE:(h + 1) * LANE] = (acc_ref[h] / ls[h]).T.astype(BF16)


def _attention(qn, qp, kn, kpe, vt, batch, seq):
    tokens = qn.shape[0]
    blk = ATTN_TILE
    nq = seq // blk
    kern = functools.partial(_attn_kernel, blk=blk)
    return pl.pallas_call(
        kern,
        name="attention",
        grid=(batch, N_HEADS // 2, nq),
        in_specs=[
            pl.BlockSpec((blk, 2 * LANE), lambda b, p, i: (b * nq + i, p)),
            pl.BlockSpec((blk, LANE), lambda b, p, i: (b * nq + i, p)),
            pl.BlockSpec((seq, 2 * LANE), lambda b, p, i: (b, p)),
            pl.BlockSpec((seq, 2 * LANE), lambda b, p, i: (b, 0)),
            pl.BlockSpec((nq, 2 * V_DIM, blk), lambda b, p, i: (b, p, 0)),
        ],
        out_specs=pl.BlockSpec((blk, 2 * LANE), lambda b, p, i: (b * nq + i, p)),
        out_shape=jax.ShapeDtypeStruct((tokens, N_HEADS * V_DIM), BF16),
        scratch_shapes=[pltpu.VMEM((2, 2, blk, blk), F32), pltpu.VMEM((2, V_DIM, blk), F32)],
        compiler_params=_params(("arbitrary", "arbitrary", "arbitrary")),
    )(qn, qp, kn, kpe, vt)


def _out_proj_kernel(x_ref, a_ref, w_ref, o_ref):
    o_ref[...] = x_ref[...] + jnp.dot(a_ref[...], w_ref[...], preferred_element_type=F32)


def _out_proj(x, attn, w_o):
    tokens, d = x.shape
    tm = ROW_TILE
    return pl.pallas_call(
        _out_proj_kernel,
        name="out_proj",
        grid=(tokens // tm,),
        in_specs=[
            pl.BlockSpec((tm, d), lambda i: (i, 0)),
            pl.BlockSpec((tm, attn.shape[1]), lambda i: (i, 0)),
            pl.BlockSpec(w_o.shape, lambda i: (0, 0)),
        ],
        out_specs=pl.BlockSpec((tm, d), lambda i: (i, 0)),
        out_shape=jax.ShapeDtypeStruct((tokens, d), F32),
        compiler_params=_params(("arbitrary",)),
    )(x, attn, w_o)


def _swap_halves(w):
    half = w.shape[-1] // 2
    return jnp.concatenate([w[..., half:], w[..., :half]], axis=-1)


def _rope_tables(positions):
    half = QK_ROPE_DIM // 2
    inv_freq = ROPE_BASE ** (-jnp.arange(half, dtype=F32) / half)
    ang = positions.astype(F32).reshape(-1, 1) * inv_freq
    cos, sin = jnp.cos(ang), jnp.sin(ang)
    return (jnp.concatenate([cos, cos, cos, cos], axis=-1),
            jnp.concatenate([-sin, sin, -sin, sin], axis=-1))


def kernel(x, positions, pool_norm, pool_w, pool_scale, kv_in_norm, w_kv_a, kv_latent_norm, w_kv_b, attn_norm, w_q_a, q_latent_norm, w_q_b, w_o, ffn_norm, w_gate, w_up, w_down, final_norm):
    batch, seq, d = x.shape
    depth = ffn_norm.shape[0]
    n_pool = pool_norm.shape[0]
    assert seq % ROW_TILE == 0 and seq % ATTN_TILE == 0
    assert w_gate.shape[2] % FF_TILE == 0

    cos, sin = _rope_tables(positions)
    row = lambda v: v.reshape(1, -1)
    xs = x.reshape(batch * seq, d)

    wg, wu, wd = w_gate.astype(BF16), w_up.astype(BF16), w_down.astype(BF16)
    zpad = jnp.zeros((d, LANE - QK_ROPE_DIM), F32)
    w_pe = w_kv_a[:, KV_LORA_RANK:]
    wa_ext = jnp.concatenate([w_kv_a[:, :KV_LORA_RANK], w_pe, zpad, _swap_halves(w_pe), zpad], axis=1).astype(BF16)
    wkb = w_kv_b.reshape(KV_LORA_RANK, N_HEADS, QK_NOPE_DIM + V_DIM)
    w_kn = wkb[:, :, :QK_NOPE_DIM].reshape(KV_LORA_RANK, -1).astype(BF16)
    w_vt = wkb[:, :, QK_NOPE_DIM:].reshape(KV_LORA_RANK, -1).T.astype(BF16)

    kn = vt = kpe = None
    for l in range(depth):
        if l == n_pool:
            kn, vt, kpe = _kv_proj(xs, row(kv_in_norm), wa_ext, row(kv_latent_norm), w_kn, w_vt, cos, sin)
        if l < n_pool:
            xs = _pool_mixer(xs, seq, row(pool_norm[l]), pool_w[l].astype(BF16), row(pool_scale[l]))
        else:
            b = l - n_pool
            rank = w_q_b.shape[1]
            wqb = w_q_b[b].reshape(rank, N_HEADS, QK_DIM)
            wqb_pe = wqb[:, :, QK_NOPE_DIM:]
            wqb_ext = jnp.concatenate([
                wqb[:, :, :QK_NOPE_DIM].reshape(rank, -1),
                wqb_pe.reshape(rank, -1),
                _swap_halves(wqb_pe).reshape(rank, -1)], axis=1).astype(BF16)
            qn, qp = _q_proj(xs, row(attn_norm[b]), w_q_a[b].astype(BF16), row(q_latent_norm[b]), wqb_ext, cos, sin)
            attn = _attention(qn, qp, kn, kpe, vt, batch, seq)
            xs = _out_proj(xs, attn, w_o[b].astype(BF16))
        last = l == depth - 1
        xs = _ffn(xs, row(ffn_norm[l]), wg[l], wu[l], wd[l], row(final_norm), final_norm=last)
    return xs.reshape(batch, seq, d)
```

```python
import functools
import math

import jax
import jax.numpy as jnp
from jax import lax
from jax.experimental import pallas as pl
from jax.experimental.pallas import tpu as pltpu

N_HEADS = 16
QK_NOPE_DIM = 128
QK_ROPE_DIM = 64
QK_DIM = QK_NOPE_DIM + QK_ROPE_DIM
V_DIM = 128
KV_LORA_RANK = 512
POOL_WINDOWS = (2, 4, 8, 16)
ROPE_BASE = 10000.0
NORM_EPS = 1e-6

LANE = 128
HALO = max(POOL_WINDOWS)
Q_PRESCALE = (1.0 / math.sqrt(QK_DIM)) * math.log2(math.e)

ROW_TILE = 512
FF_TILE = 512
ATTN_TILE = 512
VMEM_LIMIT = 56 * 1024 * 1024

F32 = jnp.float32
BF16 = jnp.bfloat16


def _rms(x, g):
    ms = jnp.mean(x * x, axis=-1, keepdims=True)
    return x * lax.rsqrt(ms + NORM_EPS) * g


def _params(semantics):
    return pltpu.CompilerParams(dimension_semantics=semantics, vmem_limit_bytes=VMEM_LIMIT)


def _pool_kernel(x_ref, halo_ref, g_ref, w_ref, sc_ref, o_ref, hh_ref, *, tiles_per_seq, tm, pg):
    t = pl.program_id(0) % tiles_per_seq
    g = g_ref[...]
    hh_ref[HALO:, :] = _rms(x_ref[...], g)
    hh_ref[:HALO, :] = jnp.where(t == 0, 0.0, _rms(halo_ref[...], g))
    pos = t * tm + lax.broadcasted_iota(jnp.int32, (tm, 1), 0)
    for gi, w in enumerate(POOL_WINDOWS):
        cols = slice(gi * pg, (gi + 1) * pg)
        h = hh_ref[HALO:, cols]
        acc = h
        for j in range(1, w):
            acc = acc + hh_ref[HALO - j:HALO - j + tm, cols]
        count = jnp.minimum(pos + 1, w).astype(F32)
        diff = acc / count - h
        y = jnp.dot(diff.astype(BF16), w_ref[gi], preferred_element_type=F32)
        o_ref[:, cols] = x_ref[:, cols] + y * sc_ref[:, cols]


def _pool_mixer(x, seq, norm, w, scale):
    tokens, d = x.shape
    tm = ROW_TILE
    pg = d // len(POOL_WINDOWS)
    kern = functools.partial(_pool_kernel, tiles_per_seq=seq // tm, tm=tm, pg=pg)
    return pl.pallas_call(
        kern,
        name="pool_mixer",
        grid=(tokens // tm,),
        in_specs=[
            pl.BlockSpec((tm, d), lambda i: (i, 0)),
            pl.BlockSpec((HALO, d), lambda i: (jnp.maximum(i * (tm // HALO) - 1, 0), 0)),
            pl.BlockSpec((1, d), lambda i: (0, 0)),
            pl.BlockSpec(w.shape, lambda i: (0, 0, 0)),
            pl.BlockSpec((1, d), lambda i: (0, 0)),
        ],
        out_specs=pl.BlockSpec((tm, d), lambda i: (i, 0)),
        out_shape=jax.ShapeDtypeStruct((tokens, d), F32),
        scratch_shapes=[pltpu.VMEM((HALO + tm, d), F32)],
        compiler_params=_params(("arbitrary",)),
    )(x, x, norm, w, scale)


def _ffn_kernel(x_ref, g_ref, wg_ref, wu_ref, wd_ref, fg_ref, o_ref, hn_ref, *, final_norm):
    j = pl.program_id(1)

    @pl.when(j == 0)
    def _():
        x = x_ref[...]
        hn_ref[...] = _rms(x, g_ref[...]).astype(BF16)
        o_ref[...] = x

    hn = hn_ref[...]
    gate = jnp.dot(hn, wg_ref[...], preferred_element_type=F32)
    up = jnp.dot(hn, wu_ref[...], preferred_element_type=F32)
    act = gate / (1.0 + jnp.exp(-gate)) * up
    o_ref[...] += jnp.dot(act.astype(BF16), wd_ref[...], preferred_element_type=F32)

    if final_norm:
        @pl.when(j == pl.num_programs(1) - 1)
        def _():
            o_ref[...] = _rms(o_ref[...], fg_ref[...])


def _ffn(x, norm, w_gate, w_up, w_down, final_g, *, final_norm):
    tokens, d = x.shape
    dff = w_gate.shape[1]
    tm, tf = ROW_TILE, FF_TILE
    kern = functools.partial(_ffn_kernel, final_norm=final_norm)
    return pl.pallas_call(
        kern,
        name="ffn_final" if final_norm else "ffn",
        grid=(tokens // tm, dff // tf),
        in_specs=[
            pl.BlockSpec((tm, d), lambda i, j: (i, 0)),
            pl.BlockSpec((1, d), lambda i, j: (0, 0)),
            pl.BlockSpec((d, tf), lambda i, j: (0, j)),
            pl.BlockSpec((d, tf), lambda i, j: (0, j)),
            pl.BlockSpec((tf, d), lambda i, j: (j, 0)),
            pl.BlockSpec((1, d), lambda i, j: (0, 0)),
        ],
        out_specs=pl.BlockSpec((tm, d), lambda i, j: (i, 0)),
        out_shape=jax.ShapeDtypeStruct((tokens, d), F32),
        scratch_shapes=[pltpu.VMEM((tm, d), BF16)],
        compiler_params=_params(("arbitrary", "arbitrary")),
    )(x, norm, w_gate, w_up, w_down, final_g)


def _kv_kernel(x_ref, g_ref, wa_ref, gl_ref, wkn_ref, wvt_ref, cos_ref, sin_ref, kn_ref, vt_ref, kpe_ref):
    h = _rms(x_ref[...], g_ref[...]).astype(BF16)
    kv = jnp.dot(h, wa_ref[...], preferred_element_type=F32)
    c_kv = _rms(kv[:, :KV_LORA_RANK], gl_ref[...]).astype(BF16)
    pe = kv[:, KV_LORA_RANK:KV_LORA_RANK + LANE]
    pe_swapped = kv[:, KV_LORA_RANK + LANE:]
    rot = pe * cos_ref[...] + pe_swapped * sin_ref[...]
    kpe_ref[:, :LANE] = rot.astype(BF16)
    kpe_ref[:, LANE:] = pltpu.roll(rot, LANE // 2, axis=1).astype(BF16)
    kn_ref[...] = jnp.dot(c_kv, wkn_ref[...], preferred_element_type=F32).astype(BF16)
    vt = lax.dot_general(wvt_ref[...], c_kv, (((1,), (1,)), ((), ())), preferred_element_type=F32)
    vt_ref[0] = vt.astype(BF16)


def _kv_proj(x, norm, wa_ext, latent_norm, w_kn, w_vt, cos, sin):
    tokens, d = x.shape
    tm = ATTN_TILE
    n_kn, n_v = w_kn.shape[1], w_vt.shape[0]
    return pl.pallas_call(
        _kv_kernel,
        name="kv_proj",
        grid=(tokens // tm,),
        in_specs=[
            pl.BlockSpec((tm, d), lambda i: (i, 0)),
            pl.BlockSpec((1, d), lambda i: (0, 0)),
            pl.BlockSpec(wa_ext.shape, lambda i: (0, 0)),
            pl.BlockSpec((1, KV_LORA_RANK), lambda i: (0, 0)),
            pl.BlockSpec(w_kn.shape, lambda i: (0, 0)),
            pl.BlockSpec(w_vt.shape, lambda i: (0, 0)),
            pl.BlockSpec((tm, LANE), lambda i: (i, 0)),
            pl.BlockSpec((tm, LANE), lambda i: (i, 0)),
        ],
        out_specs=[
            pl.BlockSpec((tm, n_kn), lambda i: (i, 0)),
            pl.BlockSpec((1, n_v, tm), lambda i: (i, 0, 0)),
            pl.BlockSpec((tm, 2 * LANE), lambda i: (i, 0)),
        ],
        out_shape=[
            jax.ShapeDtypeStruct((tokens, n_kn), BF16),
            jax.ShapeDtypeStruct((tokens // tm, n_v, tm), BF16),
            jax.ShapeDtypeStruct((tokens, 2 * LANE), BF16),
        ],
        compiler_params=_params(("arbitrary",)),
    )(x, norm, wa_ext, latent_norm, w_kn, w_vt, cos, sin)


def _q_kernel(x_ref, g_ref, wa_ref, gl_ref, wb_ref, cos_ref, sin_ref, qn_ref, qp_ref, *, n_nope, n_pe):
    h = _rms(x_ref[...], g_ref[...]).astype(BF16)
    cq = jnp.dot(h, wa_ref[...], preferred_element_type=F32)
    cqn = _rms(cq, gl_ref[...]).astype(BF16)
    q = jnp.dot(cqn, wb_ref[...], preferred_element_type=F32)
    qn_ref[...] = (q[:, :n_nope] * Q_PRESCALE).astype(BF16)
    cos = cos_ref[...] * Q_PRESCALE
    sin = sin_ref[...] * Q_PRESCALE
    for p in range(n_pe // LANE):
        pe = q[:, n_nope + p * LANE:n_nope + (p + 1) * LANE]
        sw = q[:, n_nope + n_pe + p * LANE:n_nope + n_pe + (p + 1) * LANE]
        qp_ref[:, p * LANE:(p + 1) * LANE] = (pe * cos + sw * sin).astype(BF16)


def _q_proj(x, norm, wa, latent_norm, wb_ext, cos, sin):
    tokens, d = x.shape
    tm = ROW_TILE
    n_nope = N_HEADS * QK_NOPE_DIM
    n_pe = N_HEADS * QK_ROPE_DIM
    kern = functools.partial(_q_kernel, n_nope=n_nope, n_pe=n_pe)
    return pl.pallas_call(
        kern,
        name="q_proj",
        grid=(tokens // tm,),
        in_specs=[
            pl.BlockSpec((tm, d), lambda i: (i, 0)),
            pl.BlockSpec((1, d), lambda i: (0, 0)),
            pl.BlockSpec(wa.shape, lambda i: (0, 0)),
            pl.BlockSpec((1, wa.shape[1]), lambda i: (0, 0)),
            pl.BlockSpec(wb_ext.shape, lambda i: (0, 0)),
            pl.BlockSpec((tm, LANE), lambda i: (i, 0)),
            pl.BlockSpec((tm, LANE), lambda i: (i, 0)),
        ],
        out_specs=[
            pl.BlockSpec((tm, n_nope), lambda i: (i, 0)),
            pl.BlockSpec((tm, n_pe), lambda i: (i, 0)),
        ],
        out_shape=[
            jax.ShapeDtypeStruct((tokens, n_nope), BF16),
            jax.ShapeDtypeStruct((tokens, n_pe), BF16),
        ],
        compiler_params=_params(("arbitrary",)),
    )(x, norm, wa, latent_norm, wb_ext, cos, sin)


def _attn_kernel(qn_ref, qp_ref, kn_ref, kp_ref, vt_ref, o_ref, st_ref, acc_ref, *, blk):
    qi = pl.program_id(2)
    d0 = 2 * qi
    qp = qp_ref[...]
    qs = [jnp.concatenate([qn_ref[:, h * LANE:(h + 1) * LANE], qp], axis=1) for h in range(2)]
    acc_ref[...] = jnp.zeros(acc_ref.shape, F32)

    def scores(h, j, q):
        off = pl.multiple_of(j * blk, blk)
        k = jnp.concatenate([kn_ref[pl.ds(off, blk), h * LANE:(h + 1) * LANE],
                             kp_ref[pl.ds(off, blk), h * LANE:(h + 1) * LANE]], axis=1)
        return lax.dot_general(k, q, (((1,), (1,)), ((), ())), preferred_element_type=F32)

    def consume(h, s, slot, vblk, m, l, cmax):
        m_new = jnp.maximum(m, cmax)
        alpha = jnp.exp2(m - m_new)
        pt = jnp.exp2(st_ref[h, s, slot] - m_new)
        l = alpha * l + jnp.sum(pt, axis=0, keepdims=True)
        vt = vt_ref[vblk, h * V_DIM:(h + 1) * V_DIM, :]
        acc_ref[h, s] = alpha * acc_ref[h, s] + jnp.dot(vt, pt.astype(BF16), preferred_element_type=F32)
        return m_new, l

    def colmax(st):
        return jnp.max(st, axis=0, keepdims=True)

    def step(t, carry, slot_a):
        vblk = jnp.where(t == 0, d0, t - 1)
        out = []
        for h in range(2):
            st = scores(h, t, qs[h])
            for s in range(2):
                slot = slot_a if s == 0 else 1 - slot_a
                m, l, cmax = carry[2 * h + s]
                st_s = st[:, s * blk:(s + 1) * blk]
                st_ref[h, s, 1 - slot] = st_s
                m, l = consume(h, s, slot, vblk, m, l, cmax)
                out.append((m, l, colmax(st_s)))
        return tuple(out)

    key = lax.broadcasted_iota(jnp.int32, (blk, blk), 0)
    qry = lax.broadcasted_iota(jnp.int32, (blk, blk), 1)
    causal = key <= qry
    m0 = jnp.full((1, blk), -jnp.inf, F32)
    l0 = jnp.zeros((1, blk), F32)
    carry = []
    for h in range(2):
        s_d0 = scores(h, d0, qs[h])
        diag_a = jnp.where(causal, s_d0[:, :blk], -jnp.inf)
        full_b = s_d0[:, blk:]
        diag_b = jnp.where(causal, scores(h, d0 + 1, qs[h][blk:, :]), -jnp.inf)
        st_ref[h, 0, 0] = diag_a
        st_ref[h, 1, 0] = diag_b
        st_ref[h, 1, 1] = full_b
        carry.append((m0, l0, colmax(diag_a)))
        m, l = consume(h, 1, 0, d0 + 1, m0, l0, colmax(diag_b))
        carry.append((m, l, colmax(full_b)))

    def pair(u, carry):
        return step(2 * u + 1, step(2 * u, carry, 0), 1)

    carry = lax.fori_loop(0, qi, pair, tuple(carry))

    vblk = jnp.maximum(d0 - 1, 0)
    for h in range(2):
        for s in range(2):
            _, l = consume(h, s, s, vblk, *carry[2 * h + s])
            o_ref[s * blk:(s + 1) * blk, h * LANE:(h + 1) * LANE] = (acc_ref[h, s] / l).T.astype(BF16)


def _attention(qn, qp, kn, kpe, vt, batch, seq):
    tokens = qn.shape[0]
    blk = ATTN_TILE
    nk = seq // blk
    nq = seq // (2 * blk)
    kern = functools.partial(_attn_kernel, blk=blk)
    return pl.pallas_call(
        kern,
        name="attention",
        grid=(batch, N_HEADS // 2, nq),
        in_specs=[
            pl.BlockSpec((2 * blk, 2 * LANE), lambda b, p, i: (b * nq + i, p)),
            pl.BlockSpec((2 * blk, LANE), lambda b, p, i: (b * nq + i, p)),
            pl.BlockSpec((seq, 2 * LANE), lambda b, p, i: (b, p)),
            pl.BlockSpec((seq, 2 * LANE), lambda b, p, i: (b, 0)),
            pl.BlockSpec((nk, 2 * V_DIM, blk), lambda b, p, i: (b, p, 0)),
        ],
        out_specs=pl.BlockSpec((2 * blk, 2 * LANE), lambda b, p, i: (b * nq + i, p)),
        out_shape=jax.ShapeDtypeStruct((tokens, N_HEADS * V_DIM), BF16),
        scratch_shapes=[pltpu.VMEM((2, 2, 2, blk, blk), F32), pltpu.VMEM((2, 2, V_DIM, blk), F32)],
        compiler_params=_params(("arbitrary", "arbitrary", "arbitrary")),
    )(qn, qp, kn, kpe, vt)


def _out_proj_kernel(x_ref, a_ref, w_ref, o_ref):
    o_ref[...] = x_ref[...] + jnp.dot(a_ref[...], w_ref[...], preferred_element_type=F32)


def _out_proj(x, attn, w_o):
    tokens, d = x.shape
    tm = ROW_TILE
    return pl.pallas_call(
        _out_proj_kernel,
        name="out_proj",
        grid=(tokens // tm,),
        in_specs=[
            pl.BlockSpec((tm, d), lambda i: (i, 0)),
            pl.BlockSpec((tm, attn.shape[1]), lambda i: (i, 0)),
            pl.BlockSpec(w_o.shape, lambda i: (0, 0)),
        ],
        out_specs=pl.BlockSpec((tm, d), lambda i: (i, 0)),
        out_shape=jax.ShapeDtypeStruct((tokens, d), F32),
        compiler_params=_params(("arbitrary",)),
    )(x, attn, w_o)


def _swap_halves(w):
    half = w.shape[-1] // 2
    return jnp.concatenate([w[..., half:], w[..., :half]], axis=-1)


def _rope_tables(positions):
    half = QK_ROPE_DIM // 2
    inv_freq = ROPE_BASE ** (-jnp.arange(half, dtype=F32) / half)
    ang = positions.astype(F32).reshape(-1, 1) * inv_freq
    cos, sin = jnp.cos(ang), jnp.sin(ang)
    return (jnp.concatenate([cos, cos, cos, cos], axis=-1),
            jnp.concatenate([-sin, sin, -sin, sin], axis=-1))


def kernel(x, positions, pool_norm, pool_w, pool_scale, kv_in_norm, w_kv_a, kv_latent_norm, w_kv_b, attn_norm, w_q_a, q_latent_norm, w_q_b, w_o, ffn_norm, w_gate, w_up, w_down, final_norm):
    batch, seq, d = x.shape
    depth = ffn_norm.shape[0]
    n_pool = pool_norm.shape[0]
    assert seq % ROW_TILE == 0 and seq % (2 * ATTN_TILE) == 0
    assert w_gate.shape[2] % FF_TILE == 0

    cos, sin = _rope_tables(positions)
    row = lambda v: v.reshape(1, -1)
    xs = x.reshape(batch * seq, d)

    wg, wu, wd = w_gate.astype(BF16), w_up.astype(BF16), w_down.astype(BF16)
    zpad = jnp.zeros((d, LANE - QK_ROPE_DIM), F32)
    w_pe = w_kv_a[:, KV_LORA_RANK:]
    wa_ext = jnp.concatenate([w_kv_a[:, :KV_LORA_RANK], w_pe, zpad, _swap_halves(w_pe), zpad], axis=1).astype(BF16)
    wkb = w_kv_b.reshape(KV_LORA_RANK, N_HEADS, QK_NOPE_DIM + V_DIM)
    w_kn = wkb[:, :, :QK_NOPE_DIM].reshape(KV_LORA_RANK, -1).astype(BF16)
    w_vt = wkb[:, :, QK_NOPE_DIM:].reshape(KV_LORA_RANK, -1).T.astype(BF16)

    kn = vt = kpe = None
    for l in range(depth):
        if l == n_pool:
            kn, vt, kpe = _kv_proj(xs, row(kv_in_norm), wa_ext, row(kv_latent_norm), w_kn, w_vt, cos, sin)
        if l < n_pool:
            xs = _pool_mixer(xs, seq, row(pool_norm[l]), pool_w[l].astype(BF16), row(pool_scale[l]))
        else:
            b = l - n_pool
            rank = w_q_b.shape[1]
            wqb = w_q_b[b].reshape(rank, N_HEADS, QK_DIM)
            wqb_pe = wqb[:, :, QK_NOPE_DIM:]
            wqb_ext = jnp.concatenate([
                wqb[:, :, :QK_NOPE_DIM].reshape(rank, -1),
                wqb_pe.reshape(rank, -1),
                _swap_halves(wqb_pe).reshape(rank, -1)], axis=1).astype(BF16)
            qn, qp = _q_proj(xs, row(attn_norm[b]), w_q_a[b].astype(BF16), row(q_latent_norm[b]), wqb_ext, cos, sin)
            attn = _attention(qn, qp, kn, kpe, vt, batch, seq)
            xs = _out_proj(xs, attn, w_o[b].astype(BF16))
        last = l == depth - 1
        xs = _ffn(xs, row(ffn_norm[l]), wg[l], wu[l], wd[l], row(final_norm), final_norm=last)
    return xs.reshape(batch, seq, d)
```

```python
import functools
import math

import jax
import jax.numpy as jnp
from jax import lax
from jax.experimental import pallas as pl
from jax.experimental.pallas import tpu as pltpu

N_HEADS = 16
QK_NOPE_DIM = 128
QK_ROPE_DIM = 64
QK_DIM = QK_NOPE_DIM + QK_ROPE_DIM
V_DIM = 128
KV_LORA_RANK = 512
POOL_WINDOWS = (2, 4, 8, 16)
ROPE_BASE = 10000.0
NORM_EPS = 1e-6

LANE = 128
HALO = max(POOL_WINDOWS)
Q_PRESCALE = (1.0 / math.sqrt(QK_DIM)) * math.log2(math.e)

ROW_TILE = 512
FF_TILE = 512
ATTN_TILE = 512
VMEM_LIMIT = 56 * 1024 * 1024

F32 = jnp.float32
BF16 = jnp.bfloat16


def _rms(x, g):
    ms = jnp.mean(x * x, axis=-1, keepdims=True)
    return x * lax.rsqrt(ms + NORM_EPS) * g


def _params(semantics, flags=None):
    return pltpu.CompilerParams(dimension_semantics=semantics, vmem_limit_bytes=VMEM_LIMIT, flags=flags)


def _pool_kernel(x_ref, halo_ref, g_ref, w_ref, sc_ref, o_ref, hh_ref, *, tiles_per_seq, tm, pg):
    t = pl.program_id(0) % tiles_per_seq
    g = g_ref[...]
    hh_ref[HALO:, :] = _rms(x_ref[...], g)
    hh_ref[:HALO, :] = jnp.where(t == 0, 0.0, _rms(halo_ref[...], g))
    pos = t * tm + lax.broadcasted_iota(jnp.int32, (tm, 1), 0)
    for gi, w in enumerate(POOL_WINDOWS):
        cols = slice(gi * pg, (gi + 1) * pg)
        h = hh_ref[HALO:, cols]
        acc = h
        for j in range(1, w):
            acc = acc + hh_ref[HALO - j:HALO - j + tm, cols]
        count = jnp.minimum(pos + 1, w).astype(F32)
        diff = acc / count - h
        y = jnp.dot(diff.astype(BF16), w_ref[gi], preferred_element_type=F32)
        o_ref[:, cols] = x_ref[:, cols] + y * sc_ref[:, cols]


def _pool_mixer(x, seq, norm, w, scale):
    tokens, d = x.shape
    tm = ROW_TILE
    pg = d // len(POOL_WINDOWS)
    kern = functools.partial(_pool_kernel, tiles_per_seq=seq // tm, tm=tm, pg=pg)
    return pl.pallas_call(
        kern,
        name="pool_mixer",
        grid=(tokens // tm,),
        in_specs=[
            pl.BlockSpec((tm, d), lambda i: (i, 0)),
            pl.BlockSpec((HALO, d), lambda i: (jnp.maximum(i * (tm // HALO) - 1, 0), 0)),
            pl.BlockSpec((1, d), lambda i: (0, 0)),
            pl.BlockSpec(w.shape, lambda i: (0, 0, 0)),
            pl.BlockSpec((1, d), lambda i: (0, 0)),
        ],
        out_specs=pl.BlockSpec((tm, d), lambda i: (i, 0)),
        out_shape=jax.ShapeDtypeStruct((tokens, d), F32),
        scratch_shapes=[pltpu.VMEM((HALO + tm, d), F32)],
        compiler_params=_params(("arbitrary",)),
    )(x, x, norm, w, scale)


def _ffn_kernel(x_ref, g_ref, wg_ref, wu_ref, wd_ref, fg_ref, o_ref, hn_ref, *, final_norm):
    j = pl.program_id(1)

    @pl.when(j == 0)
    def _():
        x = x_ref[...]
        hn_ref[...] = _rms(x, g_ref[...]).astype(BF16)
        o_ref[...] = x

    hn = hn_ref[...]
    gate = jnp.dot(hn, wg_ref[...], preferred_element_type=F32)
    up = jnp.dot(hn, wu_ref[...], preferred_element_type=F32)
    act = gate / (1.0 + jnp.exp(-gate)) * up
    o_ref[...] += jnp.dot(act.astype(BF16), wd_ref[...], preferred_element_type=F32)

    if final_norm:
        @pl.when(j == pl.num_programs(1) - 1)
        def _():
            o_ref[...] = _rms(o_ref[...], fg_ref[...])


def _ffn(x, norm, w_gate, w_up, w_down, final_g, *, final_norm):
    tokens, d = x.shape
    dff = w_gate.shape[1]
    tm, tf = ROW_TILE, FF_TILE
    kern = functools.partial(_ffn_kernel, final_norm=final_norm)
    return pl.pallas_call(
        kern,
        name="ffn_final" if final_norm else "ffn",
        grid=(tokens // tm, dff // tf),
        in_specs=[
            pl.BlockSpec((tm, d), lambda i, j: (i, 0)),
            pl.BlockSpec((1, d), lambda i, j: (0, 0)),
            pl.BlockSpec((d, tf), lambda i, j: (0, j)),
            pl.BlockSpec((d, tf), lambda i, j: (0, j)),
            pl.BlockSpec((tf, d), lambda i, j: (j, 0)),
            pl.BlockSpec((1, d), lambda i, j: (0, 0)),
        ],
        out_specs=pl.BlockSpec((tm, d), lambda i, j: (i, 0)),
        out_shape=jax.ShapeDtypeStruct((tokens, d), F32),
        scratch_shapes=[pltpu.VMEM((tm, d), BF16)],
        compiler_params=_params(("arbitrary", "arbitrary")),
    )(x, norm, w_gate, w_up, w_down, final_g)


def _kv_kernel(x_ref, g_ref, wa_ref, gl_ref, wkn_ref, wvt_ref, cos_ref, sin_ref, kn_ref, vt_ref, kpe_ref):
    h = _rms(x_ref[...], g_ref[...]).astype(BF16)
    kv = jnp.dot(h, wa_ref[...], preferred_element_type=F32)
    c_kv = _rms(kv[:, :KV_LORA_RANK], gl_ref[...]).astype(BF16)
    pe = kv[:, KV_LORA_RANK:KV_LORA_RANK + LANE]
    pe_swapped = kv[:, KV_LORA_RANK + LANE:]
    rot = pe * cos_ref[...] + pe_swapped * sin_ref[...]
    kpe_ref[:, :LANE] = rot.astype(BF16)
    kpe_ref[:, LANE:] = pltpu.roll(rot, LANE // 2, axis=1).astype(BF16)
    kn_ref[...] = jnp.dot(c_kv, wkn_ref[...], preferred_element_type=F32).astype(BF16)
    vt = lax.dot_general(wvt_ref[...], c_kv, (((1,), (1,)), ((), ())), preferred_element_type=F32)
    vt_ref[0] = vt.astype(BF16)


def _kv_proj(x, norm, wa_ext, latent_norm, w_kn, w_vt, cos, sin):
    tokens, d = x.shape
    tm = ATTN_TILE
    n_kn, n_v = w_kn.shape[1], w_vt.shape[0]
    return pl.pallas_call(
        _kv_kernel,
        name="kv_proj",
        grid=(tokens // tm,),
        in_specs=[
            pl.BlockSpec((tm, d), lambda i: (i, 0)),
            pl.BlockSpec((1, d), lambda i: (0, 0)),
            pl.BlockSpec(wa_ext.shape, lambda i: (0, 0)),
            pl.BlockSpec((1, KV_LORA_RANK), lambda i: (0, 0)),
            pl.BlockSpec(w_kn.shape, lambda i: (0, 0)),
            pl.BlockSpec(w_vt.shape, lambda i: (0, 0)),
            pl.BlockSpec((tm, LANE), lambda i: (i, 0)),
            pl.BlockSpec((tm, LANE), lambda i: (i, 0)),
        ],
        out_specs=[
            pl.BlockSpec((tm, n_kn), lambda i: (i, 0)),
            pl.BlockSpec((1, n_v, tm), lambda i: (i, 0, 0)),
            pl.BlockSpec((tm, 2 * LANE), lambda i: (i, 0)),
        ],
        out_shape=[
            jax.ShapeDtypeStruct((tokens, n_kn), BF16),
            jax.ShapeDtypeStruct((tokens // tm, n_v, tm), BF16),
            jax.ShapeDtypeStruct((tokens, 2 * LANE), BF16),
        ],
        compiler_params=_params(("arbitrary",)),
    )(x, norm, wa_ext, latent_norm, w_kn, w_vt, cos, sin)


def _q_kernel(x_ref, g_ref, wa_ref, gl_ref, wb_ref, cos_ref, sin_ref, qn_ref, qp_ref, *, n_nope, n_pe):
    h = _rms(x_ref[...], g_ref[...]).astype(BF16)
    cq = jnp.dot(h, wa_ref[...], preferred_element_type=F32)
    cqn = _rms(cq, gl_ref[...]).astype(BF16)
    qt = lax.dot_general(wb_ref[...], cqn, (((1,), (1,)), ((), ())), preferred_element_type=F32)
    qn_ref[...] = (qt[:n_nope, :] * Q_PRESCALE).astype(BF16)
    cos = cos_ref[...] * Q_PRESCALE
    sin = sin_ref[...] * Q_PRESCALE
    for p in range(n_pe // LANE):
        pe = qt[n_nope + p * LANE:n_nope + (p + 1) * LANE, :]
        sw = qt[n_nope + n_pe + p * LANE:n_nope + n_pe + (p + 1) * LANE, :]
        qp_ref[p * LANE:(p + 1) * LANE, :] = (pe * cos + sw * sin).astype(BF16)


def _q_proj(x, norm, wa, latent_norm, wb_ext, cos, sin):
    tokens, d = x.shape
    tm = ROW_TILE
    n_nope = N_HEADS * QK_NOPE_DIM
    n_pe = N_HEADS * QK_ROPE_DIM
    kern = functools.partial(_q_kernel, n_nope=n_nope, n_pe=n_pe)
    return pl.pallas_call(
        kern,
        name="q_proj",
        grid=(tokens // tm,),
        in_specs=[
            pl.BlockSpec((tm, d), lambda i: (i, 0)),
            pl.BlockSpec((1, d), lambda i: (0, 0)),
            pl.BlockSpec(wa.shape, lambda i: (0, 0)),
            pl.BlockSpec((1, wa.shape[1]), lambda i: (0, 0)),
            pl.BlockSpec(wb_ext.shape, lambda i: (0, 0)),
            pl.BlockSpec((LANE, tm), lambda i: (0, i)),
            pl.BlockSpec((LANE, tm), lambda i: (0, i)),
        ],
        out_specs=[
            pl.BlockSpec((n_nope, tm), lambda i: (0, i)),
            pl.BlockSpec((n_pe, tm), lambda i: (0, i)),
        ],
        out_shape=[
            jax.ShapeDtypeStruct((n_nope, tokens), BF16),
            jax.ShapeDtypeStruct((n_pe, tokens), BF16),
        ],
        compiler_params=_params(("arbitrary",)),
    )(x, norm, wa, latent_norm, wb_ext, cos, sin)


def _attn_kernel(qn_ref, qp_ref, kn_ref, kp_ref, vt_ref, o_ref, st_ref, acc_ref, *, blk):
    qi = pl.program_id(2)
    d0 = 2 * qi
    qp = qp_ref[...]
    qts = [jnp.concatenate([qn_ref[h * LANE:(h + 1) * LANE, :], qp], axis=0) for h in range(2)]
    acc_ref[...] = jnp.zeros(acc_ref.shape, F32)

    def scores(h, j, qt):
        off = pl.multiple_of(j * blk, blk)
        k = jnp.concatenate([kn_ref[pl.ds(off, blk), h * LANE:(h + 1) * LANE],
                             kp_ref[pl.ds(off, blk), h * LANE:(h + 1) * LANE]], axis=1)
        return jnp.dot(k, qt, preferred_element_type=F32)

    def consume(h, s, slot, vblk, m, l, cmax):
        m_new = jnp.maximum(m, cmax)
        alpha = jnp.exp2(m - m_new)
        pt = jnp.exp2(st_ref[h, s, slot] - m_new)
        l = alpha * l + jnp.sum(pt, axis=0, keepdims=True)
        vt = vt_ref[vblk, h * V_DIM:(h + 1) * V_DIM, :]
        acc_ref[h, s] = alpha * acc_ref[h, s] + jnp.dot(vt, pt.astype(BF16), preferred_element_type=F32)
        return m_new, l

    def colmax(st):
        return jnp.max(st, axis=0, keepdims=True)

    def step(t, carry, slot_a):
        vblk = jnp.where(t == 0, d0, t - 1)
        out = []
        for h in range(2):
            st = scores(h, t, qts[h])
            for s in range(2):
                slot = slot_a if s == 0 else 1 - slot_a
                m, l, cmax = carry[2 * h + s]
                st_s = st[:, s * blk:(s + 1) * blk]
                st_ref[h, s, 1 - slot] = st_s
                m, l = consume(h, s, slot, vblk, m, l, cmax)
                out.append((m, l, colmax(st_s)))
        return tuple(out)

    key = lax.broadcasted_iota(jnp.int32, (blk, blk), 0)
    qry = lax.broadcasted_iota(jnp.int32, (blk, blk), 1)
    causal = key <= qry
    m0 = jnp.full((1, blk), -jnp.inf, F32)
    l0 = jnp.zeros((1, blk), F32)
    carry = []
    for h in range(2):
        s_d0 = scores(h, d0, qts[h])
        diag_a = jnp.where(causal, s_d0[:, :blk], -jnp.inf)
        full_b = s_d0[:, blk:]
        diag_b = jnp.where(causal, scores(h, d0 + 1, qts[h][:, blk:]), -jnp.inf)
        st_ref[h, 0, 0] = diag_a
        st_ref[h, 1, 0] = diag_b
        st_ref[h, 1, 1] = full_b
        carry.append((m0, l0, colmax(diag_a)))
        m, l = consume(h, 1, 0, d0 + 1, m0, l0, colmax(diag_b))
        carry.append((m, l, colmax(full_b)))

    def pair(u, carry):
        return step(2 * u + 1, step(2 * u, carry, 0), 1)

    carry = lax.fori_loop(0, qi, pair, tuple(carry))

    vblk = jnp.maximum(d0 - 1, 0)
    for h in range(2):
        for s in range(2):
            _, l = consume(h, s, s, vblk, *carry[2 * h + s])
            o_ref[s * blk:(s + 1) * blk, h * LANE:(h + 1) * LANE] = (acc_ref[h, s] / l).T.astype(BF16)


def _attention(qn, qp, kn, kpe, vt, batch, seq):
    tokens = qn.shape[1]
    blk = ATTN_TILE
    nk = seq // blk
    nq = seq // (2 * blk)
    kern = functools.partial(_attn_kernel, blk=blk)
    return pl.pallas_call(
        kern,
        name="attention",
        grid=(batch, N_HEADS // 2, nq),
        in_specs=[
            pl.BlockSpec((2 * LANE, 2 * blk), lambda b, p, i: (p, b * nq + i)),
            pl.BlockSpec((LANE, 2 * blk), lambda b, p, i: (p, b * nq + i)),
            pl.BlockSpec((seq, 2 * LANE), lambda b, p, i: (b, p)),
            pl.BlockSpec((seq, 2 * LANE), lambda b, p, i: (b, 0)),
            pl.BlockSpec((nk, 2 * V_DIM, blk), lambda b, p, i: (b, p, 0)),
        ],
        out_specs=pl.BlockSpec((2 * blk, 2 * LANE), lambda b, p, i: (b * nq + i, p)),
        out_shape=jax.ShapeDtypeStruct((tokens, N_HEADS * V_DIM), BF16),
        scratch_shapes=[pltpu.VMEM((2, 2, 2, blk, blk), F32), pltpu.VMEM((2, 2, V_DIM, blk), F32)],
        compiler_params=_params(("arbitrary", "arbitrary", "arbitrary")),
    )(qn, qp, kn, kpe, vt)


def _out_proj_kernel(x_ref, a_ref, w_ref, o_ref):
    o_ref[...] = x_ref[...] + jnp.dot(a_ref[...], w_ref[...], preferred_element_type=F32)


def _out_proj(x, attn, w_o):
    tokens, d = x.shape
    tm = ROW_TILE
    return pl.pallas_call(
        _out_proj_kernel,
        name="out_proj",
        grid=(tokens // tm,),
        in_specs=[
            pl.BlockSpec((tm, d), lambda i: (i, 0)),
            pl.BlockSpec((tm, attn.shape[1]), lambda i: (i, 0)),
            pl.BlockSpec(w_o.shape, lambda i: (0, 0)),
        ],
        out_specs=pl.BlockSpec((tm, d), lambda i: (i, 0)),
        out_shape=jax.ShapeDtypeStruct((tokens, d), F32),
        compiler_params=_params(("arbitrary",)),
    )(x, attn, w_o)


def _swap_halves(w):
    half = w.shape[-1] // 2
    return jnp.concatenate([w[..., half:], w[..., :half]], axis=-1)


def _rope_tables(positions):
    half = QK_ROPE_DIM // 2
    inv_freq = ROPE_BASE ** (-jnp.arange(half, dtype=F32) / half)
    ang = positions.astype(F32).reshape(-1, 1) * inv_freq
    cos, sin = jnp.cos(ang), jnp.sin(ang)
    return (jnp.concatenate([cos, cos, cos, cos], axis=-1),
            jnp.concatenate([-sin, sin, -sin, sin], axis=-1))


def kernel(x, positions, pool_norm, pool_w, pool_scale, kv_in_norm, w_kv_a, kv_latent_norm, w_kv_b, attn_norm, w_q_a, q_latent_norm, w_q_b, w_o, ffn_norm, w_gate, w_up, w_down, final_norm):
    batch, seq, d = x.shape
    depth = ffn_norm.shape[0]
    n_pool = pool_norm.shape[0]
    assert seq % ROW_TILE == 0 and seq % (2 * ATTN_TILE) == 0
    assert w_gate.shape[2] % FF_TILE == 0

    cos, sin = _rope_tables(positions)
    row = lambda v: v.reshape(1, -1)
    xs = x.reshape(batch * seq, d)

    wg, wu, wd = w_gate.astype(BF16), w_up.astype(BF16), w_down.astype(BF16)
    zpad = jnp.zeros((d, LANE - QK_ROPE_DIM), F32)
    w_pe = w_kv_a[:, KV_LORA_RANK:]
    wa_ext = jnp.concatenate([w_kv_a[:, :KV_LORA_RANK], w_pe, zpad, _swap_halves(w_pe), zpad], axis=1).astype(BF16)
    wkb = w_kv_b.reshape(KV_LORA_RANK, N_HEADS, QK_NOPE_DIM + V_DIM)
    w_kn = wkb[:, :, :QK_NOPE_DIM].reshape(KV_LORA_RANK, -1).astype(BF16)
    w_vt = wkb[:, :, QK_NOPE_DIM:].reshape(KV_LORA_RANK, -1).T.astype(BF16)

    kn = vt = kpe = None
    for l in range(depth):
        if l == n_pool:
            kn, vt, kpe = _kv_proj(xs, row(kv_in_norm), wa_ext, row(kv_latent_norm), w_kn, w_vt, cos, sin)
        if l < n_pool:
            xs = _pool_mixer(xs, seq, row(pool_norm[l]), pool_w[l].astype(BF16), row(pool_scale[l]))
        else:
            b = l - n_pool
            rank = w_q_b.shape[1]
            wqb = w_q_b[b].reshape(rank, N_HEADS, QK_DIM)
            wqb_pe = wqb[:, :, QK_NOPE_DIM:]
            wqb_ext = jnp.concatenate([
                wqb[:, :, :QK_NOPE_DIM].reshape(rank, -1),
                wqb_pe.reshape(rank, -1),
                _swap_halves(wqb_pe).reshape(rank, -1)], axis=1).T.astype(BF16)
            qn, qp = _q_proj(xs, row(attn_norm[b]), w_q_a[b].astype(BF16), row(q_latent_norm[b]), wqb_ext, cos.T, sin.T)
            attn = _attention(qn, qp, kn, kpe, vt, batch, seq)
            xs = _out_proj(xs, attn, w_o[b].astype(BF16))
        last = l == depth - 1
        xs = _ffn(xs, row(ffn_norm[l]), wg[l], wu[l], wd[l], row(final_norm), final_norm=last)
    return xs.reshape(batch, seq, d)
```

```python
import functools
import math

import jax
import jax.numpy as jnp
from jax import lax
from jax.experimental import pallas as pl
from jax.experimental.pallas import tpu as pltpu

N_HEADS = 16
QK_NOPE_DIM = 128
QK_ROPE_DIM = 64
QK_DIM = QK_NOPE_DIM + QK_ROPE_DIM
V_DIM = 128
KV_LORA_RANK = 512
POOL_WINDOWS = (2, 4, 8, 16)
ROPE_BASE = 10000.0
NORM_EPS = 1e-6

LANE = 128
HALO = max(POOL_WINDOWS)
Q_PRESCALE = (1.0 / math.sqrt(QK_DIM)) * math.log2(math.e)

ROW_TILE = 512
FF_TILE = 512
ATTN_TILE = 512
VMEM_LIMIT = 56 * 1024 * 1024

F32 = jnp.float32
BF16 = jnp.bfloat16


def _rms(x, g):
    ms = jnp.mean(x * x, axis=-1, keepdims=True)
    return x * lax.rsqrt(ms + NORM_EPS) * g


def _params(semantics, flags=None):
    return pltpu.CompilerParams(dimension_semantics=semantics, vmem_limit_bytes=VMEM_LIMIT, flags=flags)


def _pool_kernel(x_ref, halo_ref, g_ref, w_ref, sc_ref, o_ref, hh_ref, *, tiles_per_seq, tm, pg):
    t = pl.program_id(0) % tiles_per_seq
    g = g_ref[...]
    hh_ref[HALO:, :] = _rms(x_ref[...], g)
    hh_ref[:HALO, :] = jnp.where(t == 0, 0.0, _rms(halo_ref[...], g))
    pos = t * tm + lax.broadcasted_iota(jnp.int32, (tm, 1), 0)
    for gi, w in enumerate(POOL_WINDOWS):
        cols = slice(gi * pg, (gi + 1) * pg)
        h = hh_ref[HALO:, cols]
        acc = h
        for j in range(1, w):
            acc = acc + hh_ref[HALO - j:HALO - j + tm, cols]
        count = jnp.minimum(pos + 1, w).astype(F32)
        diff = acc / count - h
        y = jnp.dot(diff.astype(BF16), w_ref[gi], preferred_element_type=F32)
        o_ref[:, cols] = x_ref[:, cols] + y * sc_ref[:, cols]


def _pool_mixer(x, seq, norm, w, scale):
    tokens, d = x.shape
    tm = ROW_TILE
    pg = d // len(POOL_WINDOWS)
    kern = functools.partial(_pool_kernel, tiles_per_seq=seq // tm, tm=tm, pg=pg)
    return pl.pallas_call(
        kern,
        name="pool_mixer",
        grid=(tokens // tm,),
        in_specs=[
            pl.BlockSpec((tm, d), lambda i: (i, 0)),
            pl.BlockSpec((HALO, d), lambda i: (jnp.maximum(i * (tm // HALO) - 1, 0), 0)),
            pl.BlockSpec((1, d), lambda i: (0, 0)),
            pl.BlockSpec(w.shape, lambda i: (0, 0, 0)),
            pl.BlockSpec((1, d), lambda i: (0, 0)),
        ],
        out_specs=pl.BlockSpec((tm, d), lambda i: (i, 0)),
        out_shape=jax.ShapeDtypeStruct((tokens, d), F32),
        scratch_shapes=[pltpu.VMEM((HALO + tm, d), F32)],
        compiler_params=_params(("arbitrary",)),
    )(x, x, norm, w, scale)


def _ffn_kernel(x_ref, g_ref, wg_ref, wu_ref, wd_ref, fg_ref, o_ref, hn_ref, *, final_norm):
    j = pl.program_id(1)

    @pl.when(j == 0)
    def _():
        x = x_ref[...]
        hn_ref[...] = _rms(x, g_ref[...]).astype(BF16)
        o_ref[...] = x

    hn = hn_ref[...]
    gate = jnp.dot(hn, wg_ref[...], preferred_element_type=F32)
    up = jnp.dot(hn, wu_ref[...], preferred_element_type=F32)
    act = gate / (1.0 + jnp.exp(-gate)) * up
    o_ref[...] += jnp.dot(act.astype(BF16), wd_ref[...], preferred_element_type=F32)

    if final_norm:
        @pl.when(j == pl.num_programs(1) - 1)
        def _():
            o_ref[...] = _rms(o_ref[...], fg_ref[...])


def _ffn(x, norm, w_gate, w_up, w_down, final_g, *, layer, final_norm):
    tokens, d = x.shape
    dff = w_gate.shape[2]
    tm, tf = ROW_TILE, FF_TILE
    kern = functools.partial(_ffn_kernel, final_norm=final_norm)
    return pl.pallas_call(
        kern,
        name="ffn_final" if final_norm else "ffn",
        grid=(tokens // tm, dff // tf),
        in_specs=[
            pl.BlockSpec((tm, d), lambda i, j: (i, 0)),
            pl.BlockSpec((1, d), lambda i, j: (0, 0)),
            pl.BlockSpec((None, d, tf), lambda i, j: (layer, 0, j)),
            pl.BlockSpec((None, d, tf), lambda i, j: (layer, 0, j)),
            pl.BlockSpec((None, tf, d), lambda i, j: (layer, j, 0)),
            pl.BlockSpec((1, d), lambda i, j: (0, 0)),
        ],
        out_specs=pl.BlockSpec((tm, d), lambda i, j: (i, 0)),
        out_shape=jax.ShapeDtypeStruct((tokens, d), F32),
        scratch_shapes=[pltpu.VMEM((tm, d), BF16)],
        compiler_params=_params(("arbitrary", "arbitrary")),
    )(x, norm, w_gate, w_up, w_down, final_g)


def _kv_kernel(x_ref, g_ref, wa_ref, gl_ref, wkn_ref, wvt_ref, cos_ref, sin_ref, kn_ref, vt_ref, kpe_ref):
    h = _rms(x_ref[...], g_ref[...]).astype(BF16)
    kv = jnp.dot(h, wa_ref[...], preferred_element_type=F32)
    c_kv = _rms(kv[:, :KV_LORA_RANK], gl_ref[...]).astype(BF16)
    pe = kv[:, KV_LORA_RANK:KV_LORA_RANK + LANE]
    pe_swapped = kv[:, KV_LORA_RANK + LANE:]
    rot = pe * cos_ref[...] + pe_swapped * sin_ref[...]
    kpe_ref[:, :LANE] = rot.astype(BF16)
    kpe_ref[:, LANE:] = pltpu.roll(rot, LANE // 2, axis=1).astype(BF16)
    kn_ref[...] = jnp.dot(c_kv, wkn_ref[...], preferred_element_type=F32).astype(BF16)
    vt = lax.dot_general(wvt_ref[...], c_kv, (((1,), (1,)), ((), ())), preferred_element_type=F32)
    vt_ref[0] = vt.astype(BF16)


def _kv_proj(x, norm, wa_ext, latent_norm, w_kn, w_vt, cos, sin):
    tokens, d = x.shape
    tm = ATTN_TILE
    n_kn, n_v = w_kn.shape[1], w_vt.shape[0]
    return pl.pallas_call(
        _kv_kernel,
        name="kv_proj",
        grid=(tokens // tm,),
        in_specs=[
            pl.BlockSpec((tm, d), lambda i: (i, 0)),
            pl.BlockSpec((1, d), lambda i: (0, 0)),
            pl.BlockSpec(wa_ext.shape, lambda i: (0, 0)),
            pl.BlockSpec((1, KV_LORA_RANK), lambda i: (0, 0)),
            pl.BlockSpec(w_kn.shape, lambda i: (0, 0)),
            pl.BlockSpec(w_vt.shape, lambda i: (0, 0)),
            pl.BlockSpec((tm, LANE), lambda i: (i, 0)),
            pl.BlockSpec((tm, LANE), lambda i: (i, 0)),
        ],
        out_specs=[
            pl.BlockSpec((tm, n_kn), lambda i: (i, 0)),
            pl.BlockSpec((1, n_v, tm), lambda i: (i, 0, 0)),
            pl.BlockSpec((tm, 2 * LANE), lambda i: (i, 0)),
        ],
        out_shape=[
            jax.ShapeDtypeStruct((tokens, n_kn), BF16),
            jax.ShapeDtypeStruct((tokens // tm, n_v, tm), BF16),
            jax.ShapeDtypeStruct((tokens, 2 * LANE), BF16),
        ],
        compiler_params=_params(("arbitrary",)),
    )(x, norm, wa_ext, latent_norm, w_kn, w_vt, cos, sin)


def _q_kernel(x_ref, g_ref, wa_ref, gl_ref, wb_ref, cos_ref, sin_ref, qn_ref, qp_ref, *, n_nope, n_pe):
    h = _rms(x_ref[...], g_ref[...]).astype(BF16)
    cq = jnp.dot(h, wa_ref[...], preferred_element_type=F32)
    cqn = _rms(cq, gl_ref[...]).astype(BF16)
    qt = lax.dot_general(wb_ref[...], cqn, (((1,), (1,)), ((), ())), preferred_element_type=F32)
    qn_ref[...] = (qt[:n_nope, :] * Q_PRESCALE).astype(BF16)
    cos = cos_ref[...] * Q_PRESCALE
    sin = sin_ref[...] * Q_PRESCALE
    for p in range(n_pe // LANE):
        pe = qt[n_nope + p * LANE:n_nope + (p + 1) * LANE, :]
        sw = qt[n_nope + n_pe + p * LANE:n_nope + n_pe + (p + 1) * LANE, :]
        qp_ref[p * LANE:(p + 1) * LANE, :] = (pe * cos + sw * sin).astype(BF16)


def _q_proj(x, norm, wa, latent_norm, wb_ext, cos, sin):
    tokens, d = x.shape
    tm = ROW_TILE
    n_nope = N_HEADS * QK_NOPE_DIM
    n_pe = N_HEADS * QK_ROPE_DIM
    kern = functools.partial(_q_kernel, n_nope=n_nope, n_pe=n_pe)
    return pl.pallas_call(
        kern,
        name="q_proj",
        grid=(tokens // tm,),
        in_specs=[
            pl.BlockSpec((tm, d), lambda i: (i, 0)),
            pl.BlockSpec((1, d), lambda i: (0, 0)),
            pl.BlockSpec(wa.shape, lambda i: (0, 0)),
            pl.BlockSpec((1, wa.shape[1]), lambda i: (0, 0)),
            pl.BlockSpec(wb_ext.shape, lambda i: (0, 0)),
            pl.BlockSpec((LANE, tm), lambda i: (0, i)),
            pl.BlockSpec((LANE, tm), lambda i: (0, i)),
        ],
        out_specs=[
            pl.BlockSpec((n_nope, tm), lambda i: (0, i)),
            pl.BlockSpec((n_pe, tm), lambda i: (0, i)),
        ],
        out_shape=[
            jax.ShapeDtypeStruct((n_nope, tokens), BF16),
            jax.ShapeDtypeStruct((n_pe, tokens), BF16),
        ],
        compiler_params=_params(("arbitrary",)),
    )(x, norm, wa, latent_norm, wb_ext, cos, sin)


def _attn_kernel(qn_ref, qp_ref, kn_ref, kp_ref, vt_ref, o_ref, st_ref, acc_ref, *, blk):
    qi = pl.program_id(2)
    d0 = 2 * qi
    qp = qp_ref[...]
    qts = [jnp.concatenate([qn_ref[h * LANE:(h + 1) * LANE, :], qp], axis=0) for h in range(2)]
    acc_ref[...] = jnp.zeros(acc_ref.shape, F32)

    def scores(h, j, qt):
        off = pl.multiple_of(j * blk, blk)
        k = jnp.concatenate([kn_ref[pl.ds(off, blk), h * LANE:(h + 1) * LANE],
                             kp_ref[pl.ds(off, blk), h * LANE:(h + 1) * LANE]], axis=1)
        return jnp.dot(k, qt, preferred_element_type=F32)

    def consume(h, s, slot, vblk, m, l, cmax):
        m_new = jnp.maximum(m, cmax)
        alpha = jnp.exp2(m - m_new)
        pt = jnp.exp2(st_ref[h, s, slot] - m_new)
        l = alpha * l + jnp.sum(pt, axis=0, keepdims=True)
        vt = vt_ref[vblk, h * V_DIM:(h + 1) * V_DIM, :]
        acc_ref[h, s] = alpha * acc_ref[h, s] + jnp.dot(vt, pt.astype(BF16), preferred_element_type=F32)
        return m_new, l

    def colmax(st):
        return jnp.max(st, axis=0, keepdims=True)

    def step(t, carry, slot_a):
        vblk = jnp.where(t == 0, d0, t - 1)
        out = []
        for h in range(2):
            st = scores(h, t, qts[h])
            for s in range(2):
                slot = slot_a if s == 0 else 1 - slot_a
                m, l, cmax = carry[2 * h + s]
                st_s = st[:, s * blk:(s + 1) * blk]
                st_ref[h, s, 1 - slot] = st_s
                m, l = consume(h, s, slot, vblk, m, l, cmax)
                out.append((m, l, colmax(st_s)))
        return tuple(out)

    key = lax.broadcasted_iota(jnp.int32, (blk, blk), 0)
    qry = lax.broadcasted_iota(jnp.int32, (blk, blk), 1)
    causal = key <= qry
    m0 = jnp.full((1, blk), -jnp.inf, F32)
    l0 = jnp.zeros((1, blk), F32)
    carry = []
    for h in range(2):
        s_d0 = scores(h, d0, qts[h])
        diag_a = jnp.where(causal, s_d0[:, :blk], -jnp.inf)
        full_b = s_d0[:, blk:]
        diag_b = jnp.where(causal, scores(h, d0 + 1, qts[h][:, blk:]), -jnp.inf)
        st_ref[h, 0, 0] = diag_a
        st_ref[h, 1, 0] = diag_b
        st_ref[h, 1, 1] = full_b
        carry.append((m0, l0, colmax(diag_a)))
        m, l = consume(h, 1, 0, d0 + 1, m0, l0, colmax(diag_b))
        carry.append((m, l, colmax(full_b)))

    def pair(u, carry):
        return step(2 * u + 1, step(2 * u, carry, 0), 1)

    carry = lax.fori_loop(0, qi, pair, tuple(carry))

    vblk = jnp.maximum(d0 - 1, 0)
    for h in range(2):
        for s in range(2):
            _, l = consume(h, s, s, vblk, *carry[2 * h + s])
            o_ref[s * blk:(s + 1) * blk, h * LANE:(h + 1) * LANE] = (acc_ref[h, s] / l).T.astype(BF16)


def _attention(qn, qp, kn, kpe, vt, batch, seq):
    tokens = qn.shape[1]
    blk = ATTN_TILE
    nk = seq // blk
    nq = seq // (2 * blk)
    kern = functools.partial(_attn_kernel, blk=blk)
    return pl.pallas_call(
        kern,
        name="attention",
        grid=(batch, N_HEADS // 2, nq),
        in_specs=[
            pl.BlockSpec((2 * LANE, 2 * blk), lambda b, p, i: (p, b * nq + i)),
            pl.BlockSpec((LANE, 2 * blk), lambda b, p, i: (p, b * nq + i)),
            pl.BlockSpec((seq, 2 * LANE), lambda b, p, i: (b, p)),
            pl.BlockSpec((seq, 2 * LANE), lambda b, p, i: (b, 0)),
            pl.BlockSpec((nk, 2 * V_DIM, blk), lambda b, p, i: (b, p, 0)),
        ],
        out_specs=pl.BlockSpec((2 * blk, 2 * LANE), lambda b, p, i: (b * nq + i, p)),
        out_shape=jax.ShapeDtypeStruct((tokens, N_HEADS * V_DIM), BF16),
        scratch_shapes=[pltpu.VMEM((2, 2, 2, blk, blk), F32), pltpu.VMEM((2, 2, V_DIM, blk), F32)],
        compiler_params=_params(("arbitrary", "arbitrary", "arbitrary")),
    )(qn, qp, kn, kpe, vt)


def _out_proj_kernel(x_ref, a_ref, w_ref, o_ref):
    o_ref[...] = x_ref[...] + jnp.dot(a_ref[...], w_ref[...], preferred_element_type=F32)


def _out_proj(x, attn, w_o):
    tokens, d = x.shape
    tm = ROW_TILE
    return pl.pallas_call(
        _out_proj_kernel,
        name="out_proj",
        grid=(tokens // tm,),
        in_specs=[
            pl.BlockSpec((tm, d), lambda i: (i, 0)),
            pl.BlockSpec((tm, attn.shape[1]), lambda i: (i, 0)),
            pl.BlockSpec(w_o.shape, lambda i: (0, 0)),
        ],
        out_specs=pl.BlockSpec((tm, d), lambda i: (i, 0)),
        out_shape=jax.ShapeDtypeStruct((tokens, d), F32),
        compiler_params=_params(("arbitrary",)),
    )(x, attn, w_o)


def _swap_halves(w):
    half = w.shape[-1] // 2
    return jnp.concatenate([w[..., half:], w[..., :half]], axis=-1)


def _rope_tables(positions):
    half = QK_ROPE_DIM // 2
    inv_freq = jnp.tile(ROPE_BASE ** (-jnp.arange(half, dtype=F32) / half), LANE // half)
    sign = jnp.tile(jnp.concatenate([-jnp.ones((half,), F32), jnp.ones((half,), F32)]), LANE // QK_ROPE_DIM)
    pos = positions.astype(F32).reshape(-1)
    ang = pos[:, None] * inv_freq[None, :]
    ang_t = inv_freq[:, None] * pos[None, :]
    return (jnp.cos(ang), jnp.sin(ang) * sign[None, :], jnp.cos(ang_t), jnp.sin(ang_t) * sign[:, None])


def kernel(x, positions, pool_norm, pool_w, pool_scale, kv_in_norm, w_kv_a, kv_latent_norm, w_kv_b, attn_norm, w_q_a, q_latent_norm, w_q_b, w_o, ffn_norm, w_gate, w_up, w_down, final_norm):
    batch, seq, d = x.shape
    depth = ffn_norm.shape[0]
    n_pool = pool_norm.shape[0]
    assert seq % ROW_TILE == 0 and seq % (2 * ATTN_TILE) == 0
    assert w_gate.shape[2] % FF_TILE == 0

    cos, sin, cos_t, sin_t = _rope_tables(positions)
    row = lambda v: v.reshape(1, -1)
    xs = x.reshape(batch * seq, d)

    wg, wu, wd = w_gate.astype(BF16), w_up.astype(BF16), w_down.astype(BF16)
    zpad = jnp.zeros((d, LANE - QK_ROPE_DIM), F32)
    w_pe = w_kv_a[:, KV_LORA_RANK:]
    wa_ext = jnp.concatenate([w_kv_a[:, :KV_LORA_RANK], w_pe, zpad, _swap_halves(w_pe), zpad], axis=1).astype(BF16)
    wkb = w_kv_b.reshape(KV_LORA_RANK, N_HEADS, QK_NOPE_DIM + V_DIM)
    w_kn = wkb[:, :, :QK_NOPE_DIM].reshape(KV_LORA_RANK, -1).astype(BF16)
    w_vt = wkb[:, :, QK_NOPE_DIM:].reshape(KV_LORA_RANK, -1).T.astype(BF16)

    kn = vt = kpe = None
    for l in range(depth):
        if l == n_pool:
            kn, vt, kpe = _kv_proj(xs, row(kv_in_norm), wa_ext, row(kv_latent_norm), w_kn, w_vt, cos, sin)
        if l < n_pool:
            xs = _pool_mixer(xs, seq, row(pool_norm[l]), pool_w[l].astype(BF16), row(pool_scale[l]))
        else:
            b = l - n_pool
            rank = w_q_b.shape[1]
            wqb = w_q_b[b].reshape(rank, N_HEADS, QK_DIM)
            wqb_pe = wqb[:, :, QK_NOPE_DIM:]
            wqb_ext = jnp.concatenate([
                wqb[:, :, :QK_NOPE_DIM].reshape(rank, -1),
                wqb_pe.reshape(rank, -1),
                _swap_halves(wqb_pe).reshape(rank, -1)], axis=1).T.astype(BF16)
            qn, qp = _q_proj(xs, row(attn_norm[b]), w_q_a[b].astype(BF16), row(q_latent_norm[b]), wqb_ext, cos_t, sin_t)
            attn = _attention(qn, qp, kn, kpe, vt, batch, seq)
            xs = _out_proj(xs, attn, w_o[b].astype(BF16))
        last = l == depth - 1
        xs = _ffn(xs, row(ffn_norm[l]), wg, wu, wd, row(final_norm), layer=l, final_norm=last)
    return xs.reshape(batch, seq, d)
```

```python
import functools
import math

import jax
import jax.numpy as jnp
from jax import lax
from jax.experimental import pallas as pl
from jax.experimental.pallas import tpu as pltpu

N_HEADS = 16
QK_NOPE_DIM = 128
QK_ROPE_DIM = 64
QK_DIM = QK_NOPE_DIM + QK_ROPE_DIM
V_DIM = 128
KV_LORA_RANK = 512
POOL_WINDOWS = (2, 4, 8, 16)
ROPE_BASE = 10000.0
NORM_EPS = 1e-6

LANE = 128
HALO = max(POOL_WINDOWS)
Q_PRESCALE = (1.0 / math.sqrt(QK_DIM)) * math.log2(math.e)

ROW_TILE = 512
FF_TILE = 512
FFN_ROW_TILE = 1024
FFN_SUB_ROWS = 512
ATTN_TILE = 512
VMEM_LIMIT = 56 * 1024 * 1024

F32 = jnp.float32
BF16 = jnp.bfloat16


def _rms(x, g):
    ms = jnp.mean(x * x, axis=-1, keepdims=True)
    return x * lax.rsqrt(ms + NORM_EPS) * g


def _params(semantics, flags=None):
    return pltpu.CompilerParams(dimension_semantics=semantics, vmem_limit_bytes=VMEM_LIMIT, flags=flags)


def _pool_kernel(x_ref, halo_ref, g_ref, w_ref, sc_ref, o_ref, hh_ref, *, tiles_per_seq, tm, pg):
    t = pl.program_id(0) % tiles_per_seq
    g = g_ref[...]
    hh_ref[HALO:, :] = _rms(x_ref[...], g)
    hh_ref[:HALO, :] = jnp.where(t == 0, 0.0, _rms(halo_ref[...], g))
    pos = t * tm + lax.broadcasted_iota(jnp.int32, (tm, 1), 0)
    for gi, w in enumerate(POOL_WINDOWS):
        cols = slice(gi * pg, (gi + 1) * pg)
        h = hh_ref[HALO:, cols]
        acc = h
        for j in range(1, w):
            acc = acc + hh_ref[HALO - j:HALO - j + tm, cols]
        count = jnp.minimum(pos + 1, w).astype(F32)
        diff = acc / count - h
        y = jnp.dot(diff.astype(BF16), w_ref[gi], preferred_element_type=F32)
        o_ref[:, cols] = x_ref[:, cols] + y * sc_ref[:, cols]


def _pool_mixer(x, seq, norm, w, scale):
    tokens, d = x.shape
    tm = ROW_TILE
    pg = d // len(POOL_WINDOWS)
    kern = functools.partial(_pool_kernel, tiles_per_seq=seq // tm, tm=tm, pg=pg)
    return pl.pallas_call(
        kern,
        name="pool_mixer",
        grid=(tokens // tm,),
        in_specs=[
            pl.BlockSpec((tm, d), lambda i: (i, 0)),
            pl.BlockSpec((HALO, d), lambda i: (jnp.maximum(i * (tm // HALO) - 1, 0), 0)),
            pl.BlockSpec((1, d), lambda i: (0, 0)),
            pl.BlockSpec(w.shape, lambda i: (0, 0, 0)),
            pl.BlockSpec((1, d), lambda i: (0, 0)),
        ],
        out_specs=pl.BlockSpec((tm, d), lambda i: (i, 0)),
        out_shape=jax.ShapeDtypeStruct((tokens, d), F32),
        scratch_shapes=[pltpu.VMEM((HALO + tm, d), F32)],
        compiler_params=_params(("arbitrary",)),
    )(x, x, norm, w, scale)


def _ffn_kernel(x_ref, g_ref, wg_ref, wu_ref, wd_ref, fg_ref, o_ref, hn_ref, *, final_norm):
    j = pl.program_id(1)

    halves = [slice(r * FFN_SUB_ROWS, (r + 1) * FFN_SUB_ROWS) for r in range(x_ref.shape[0] // FFN_SUB_ROWS)]

    @pl.when(j == 0)
    def _():
        for rows in halves:
            x = x_ref[rows, :]
            hn_ref[rows, :] = _rms(x, g_ref[...]).astype(BF16)
            o_ref[rows, :] = x

    for rows in halves:
        hn = hn_ref[rows, :]
        gate = jnp.dot(hn, wg_ref[...], preferred_element_type=F32)
        up = jnp.dot(hn, wu_ref[...], preferred_element_type=F32)
        act = gate / (1.0 + jnp.exp(-gate)) * up
        o_ref[rows, :] += jnp.dot(act.astype(BF16), wd_ref[...], preferred_element_type=F32)

    if final_norm:
        @pl.when(j == pl.num_programs(1) - 1)
        def _():
            for rows in halves:
                o_ref[rows, :] = _rms(o_ref[rows, :], fg_ref[...])


def _ffn(x, norm, w_gate, w_up, w_down, final_g, *, layer, final_norm):
    tokens, d = x.shape
    dff = w_gate.shape[2]
    tm, tf = FFN_ROW_TILE, FF_TILE
    kern = functools.partial(_ffn_kernel, final_norm=final_norm)
    return pl.pallas_call(
        kern,
        name="ffn_final" if final_norm else "ffn",
        grid=(tokens // tm, dff // tf),
        in_specs=[
            pl.BlockSpec((tm, d), lambda i, j: (i, 0)),
            pl.BlockSpec((1, d), lambda i, j: (0, 0)),
            pl.BlockSpec((None, d, tf), lambda i, j: (layer, 0, j)),
            pl.BlockSpec((None, d, tf), lambda i, j: (layer, 0, j)),
            pl.BlockSpec((None, tf, d), lambda i, j: (layer, j, 0)),
            pl.BlockSpec((1, d), lambda i, j: (0, 0)),
        ],
        out_specs=pl.BlockSpec((tm, d), lambda i, j: (i, 0)),
        out_shape=jax.ShapeDtypeStruct((tokens, d), F32),
        scratch_shapes=[pltpu.VMEM((tm, d), BF16)],
        compiler_params=_params(("arbitrary", "arbitrary")),
    )(x, norm, w_gate, w_up, w_down, final_g)


def _kv_kernel(x_ref, g_ref, wa_ref, gl_ref, wkn_ref, wvt_ref, cos_ref, sin_ref, kn_ref, vt_ref, kpe_ref):
    h = _rms(x_ref[...], g_ref[...]).astype(BF16)
    kv = jnp.dot(h, wa_ref[...], preferred_element_type=F32)
    c_kv = _rms(kv[:, :KV_LORA_RANK], gl_ref[...]).astype(BF16)
    pe = kv[:, KV_LORA_RANK:KV_LORA_RANK + LANE]
    pe_swapped = kv[:, KV_LORA_RANK + LANE:]
    rot = pe * cos_ref[...] + pe_swapped * sin_ref[...]
    kpe_ref[:, :LANE] = rot.astype(BF16)
    kpe_ref[:, LANE:] = pltpu.roll(rot, LANE // 2, axis=1).astype(BF16)
    kn_ref[...] = jnp.dot(c_kv, wkn_ref[...], preferred_element_type=F32).astype(BF16)
    vt = lax.dot_general(wvt_ref[...], c_kv, (((1,), (1,)), ((), ())), preferred_element_type=F32)
    vt_ref[0] = vt.astype(BF16)


def _kv_proj(x, norm, wa_ext, latent_norm, w_kn, w_vt, cos, sin):
    tokens, d = x.shape
    tm = ATTN_TILE
    n_kn, n_v = w_kn.shape[1], w_vt.shape[0]
    return pl.pallas_call(
        _kv_kernel,
        name="kv_proj",
        grid=(tokens // tm,),
        in_specs=[
            pl.BlockSpec((tm, d), lambda i: (i, 0)),
            pl.BlockSpec((1, d), lambda i: (0, 0)),
            pl.BlockSpec(wa_ext.shape, lambda i: (0, 0)),
            pl.BlockSpec((1, KV_LORA_RANK), lambda i: (0, 0)),
            pl.BlockSpec(w_kn.shape, lambda i: (0, 0)),
            pl.BlockSpec(w_vt.shape, lambda i: (0, 0)),
            pl.BlockSpec((tm, LANE), lambda i: (i, 0)),
            pl.BlockSpec((tm, LANE), lambda i: (i, 0)),
        ],
        out_specs=[
            pl.BlockSpec((tm, n_kn), lambda i: (i, 0)),
            pl.BlockSpec((1, n_v, tm), lambda i: (i, 0, 0)),
            pl.BlockSpec((tm, 2 * LANE), lambda i: (i, 0)),
        ],
        out_shape=[
            jax.ShapeDtypeStruct((tokens, n_kn), BF16),
            jax.ShapeDtypeStruct((tokens // tm, n_v, tm), BF16),
            jax.ShapeDtypeStruct((tokens, 2 * LANE), BF16),
        ],
        compiler_params=_params(("arbitrary",)),
    )(x, norm, wa_ext, latent_norm, w_kn, w_vt, cos, sin)


def _q_kernel(x_ref, g_ref, wa_ref, gl_ref, wb_ref, cos_ref, sin_ref, qn_ref, qp_ref, *, n_nope, n_pe):
    h = _rms(x_ref[...], g_ref[...]).astype(BF16)
    cq = jnp.dot(h, wa_ref[...], preferred_element_type=F32)
    cqn = _rms(cq, gl_ref[...]).astype(BF16)
    qt = lax.dot_general(wb_ref[...], cqn, (((1,), (1,)), ((), ())), preferred_element_type=F32)
    qn_ref[...] = (qt[:n_nope, :] * Q_PRESCALE).astype(BF16)
    cos = cos_ref[...] * Q_PRESCALE
    sin = sin_ref[...] * Q_PRESCALE
    for p in range(n_pe // LANE):
        pe = qt[n_nope + p * LANE:n_nope + (p + 1) * LANE, :]
        sw = qt[n_nope + n_pe + p * LANE:n_nope + n_pe + (p + 1) * LANE, :]
        qp_ref[p * LANE:(p + 1) * LANE, :] = (pe * cos + sw * sin).astype(BF16)


def _q_proj(x, norm, wa, latent_norm, wb_ext, cos, sin):
    tokens, d = x.shape
    tm = ROW_TILE
    n_nope = N_HEADS * QK_NOPE_DIM
    n_pe = N_HEADS * QK_ROPE_DIM
    kern = functools.partial(_q_kernel, n_nope=n_nope, n_pe=n_pe)
    return pl.pallas_call(
        kern,
        name="q_proj",
        grid=(tokens // tm,),
        in_specs=[
            pl.BlockSpec((tm, d), lambda i: (i, 0)),
            pl.BlockSpec((1, d), lambda i: (0, 0)),
            pl.BlockSpec(wa.shape, lambda i: (0, 0)),
            pl.BlockSpec((1, wa.shape[1]), lambda i: (0, 0)),
            pl.BlockSpec(wb_ext.shape, lambda i: (0, 0)),
            pl.BlockSpec((LANE, tm), lambda i: (0, i)),
            pl.BlockSpec((LANE, tm), lambda i: (0, i)),
        ],
        out_specs=[
            pl.BlockSpec((n_nope, tm), lambda i: (0, i)),
            pl.BlockSpec((n_pe, tm), lambda i: (0, i)),
        ],
        out_shape=[
            jax.ShapeDtypeStruct((n_nope, tokens), BF16),
            jax.ShapeDtypeStruct((n_pe, tokens), BF16),
        ],
        compiler_params=_params(("arbitrary",)),
    )(x, norm, wa, latent_norm, wb_ext, cos, sin)


def _attn_kernel(qn_ref, qp_ref, kn_ref, kp_ref, vt_ref, o_ref, st_ref, acc_ref, *, blk):
    qi = pl.program_id(2)
    d0 = 2 * qi
    qp = qp_ref[...]
    qts = [jnp.concatenate([qn_ref[h * LANE:(h + 1) * LANE, :], qp], axis=0) for h in range(2)]
    acc_ref[...] = jnp.zeros(acc_ref.shape, F32)

    def scores(h, j, qt):
        off = pl.multiple_of(j * blk, blk)
        k = jnp.concatenate([kn_ref[pl.ds(off, blk), h * LANE:(h + 1) * LANE],
                             kp_ref[pl.ds(off, blk), h * LANE:(h + 1) * LANE]], axis=1)
        return jnp.dot(k, qt, preferred_element_type=F32)

    def consume(h, s, slot, vblk, m, l, cmax):
        m_new = jnp.maximum(m, cmax)
        alpha = jnp.exp2(m - m_new)
        pt = jnp.exp2(st_ref[h, s, slot] - m_new)
        l = alpha * l + jnp.sum(pt, axis=0, keepdims=True)
        vt = vt_ref[vblk, h * V_DIM:(h + 1) * V_DIM, :]
        acc_ref[h, s] = alpha * acc_ref[h, s] + jnp.dot(vt, pt.astype(BF16), preferred_element_type=F32)
        return m_new, l

    def colmax(st):
        return jnp.max(st, axis=0, keepdims=True)

    def step(t, carry, slot_a):
        vblk = jnp.where(t == 0, d0, t - 1)
        out = []
        for h in range(2):
            st = scores(h, t, qts[h])
            for s in range(2):
                slot = slot_a if s == 0 else 1 - slot_a
                m, l, cmax = carry[2 * h + s]
                st_s = st[:, s * blk:(s + 1) * blk]
                st_ref[h, s, 1 - slot] = st_s
                m, l = consume(h, s, slot, vblk, m, l, cmax)
                out.append((m, l, colmax(st_s)))
        return tuple(out)

    key = lax.broadcasted_iota(jnp.int32, (blk, blk), 0)
    qry = lax.broadcasted_iota(jnp.int32, (blk, blk), 1)
    causal = key <= qry
    m0 = jnp.full((1, blk), -jnp.inf, F32)
    l0 = jnp.zeros((1, blk), F32)
    carry = []
    for h in range(2):
        s_d0 = scores(h, d0, qts[h])
        diag_a = jnp.where(causal, s_d0[:, :blk], -jnp.inf)
        full_b = s_d0[:, blk:]
        diag_b = jnp.where(causal, scores(h, d0 + 1, qts[h][:, blk:]), -jnp.inf)
        st_ref[h, 0, 0] = diag_a
        st_ref[h, 1, 0] = diag_b
        st_ref[h, 1, 1] = full_b
        carry.append((m0, l0, colmax(diag_a)))
        m, l = consume(h, 1, 0, d0 + 1, m0, l0, colmax(diag_b))
        carry.append((m, l, colmax(full_b)))

    def pair(u, carry):
        return step(2 * u + 1, step(2 * u, carry, 0), 1)

    carry = lax.fori_loop(0, qi, pair, tuple(carry))

    vblk = jnp.maximum(d0 - 1, 0)
    for h in range(2):
        for s in range(2):
            _, l = consume(h, s, s, vblk, *carry[2 * h + s])
            o_ref[s * blk:(s + 1) * blk, h * LANE:(h + 1) * LANE] = (acc_ref[h, s] / l).T.astype(BF16)


def _attention(qn, qp, kn, kpe, vt, batch, seq):
    tokens = qn.shape[1]
    blk = ATTN_TILE
    nk = seq // blk
    nq = seq // (2 * blk)
    kern = functools.partial(_attn_kernel, blk=blk)
    return pl.pallas_call(
        kern,
        name="attention",
        grid=(batch, N_HEADS // 2, nq),
        in_specs=[
            pl.BlockSpec((2 * LANE, 2 * blk), lambda b, p, i: (p, b * nq + i)),
            pl.BlockSpec((LANE, 2 * blk), lambda b, p, i: (p, b * nq + i)),
            pl.BlockSpec((seq, 2 * LANE), lambda b, p, i: (b, p)),
            pl.BlockSpec((seq, 2 * LANE), lambda b, p, i: (b, 0)),
            pl.BlockSpec((nk, 2 * V_DIM, blk), lambda b, p, i: (b, p, 0)),
        ],
        out_specs=pl.BlockSpec((2 * blk, 2 * LANE), lambda b, p, i: (b * nq + i, p)),
        out_shape=jax.ShapeDtypeStruct((tokens, N_HEADS * V_DIM), BF16),
        scratch_shapes=[pltpu.VMEM((2, 2, 2, blk, blk), F32), pltpu.VMEM((2, 2, V_DIM, blk), F32)],
        compiler_params=_params(("arbitrary", "arbitrary", "arbitrary")),
    )(qn, qp, kn, kpe, vt)


def _out_proj_kernel(x_ref, a_ref, w_ref, o_ref):
    o_ref[...] = x_ref[...] + jnp.dot(a_ref[...], w_ref[...], preferred_element_type=F32)


def _out_proj(x, attn, w_o):
    tokens, d = x.shape
    tm = ROW_TILE
    return pl.pallas_call(
        _out_proj_kernel,
        name="out_proj",
        grid=(tokens // tm,),
        in_specs=[
            pl.BlockSpec((tm, d), lambda i: (i, 0)),
            pl.BlockSpec((tm, attn.shape[1]), lambda i: (i, 0)),
            pl.BlockSpec(w_o.shape, lambda i: (0, 0)),
        ],
        out_specs=pl.BlockSpec((tm, d), lambda i: (i, 0)),
        out_shape=jax.ShapeDtypeStruct((tokens, d), F32),
        compiler_params=_params(("arbitrary",)),
    )(x, attn, w_o)


def _swap_halves(w):
    half = w.shape[-1] // 2
    return jnp.concatenate([w[..., half:], w[..., :half]], axis=-1)


def _rope_tables(positions):
    half = QK_ROPE_DIM // 2
    inv_freq = jnp.tile(ROPE_BASE ** (-jnp.arange(half, dtype=F32) / half), LANE // half)
    sign = jnp.tile(jnp.concatenate([-jnp.ones((half,), F32), jnp.ones((half,), F32)]), LANE // QK_ROPE_DIM)
    pos = positions.astype(F32).reshape(-1)
    ang = pos[:, None] * inv_freq[None, :]
    ang_t = inv_freq[:, None] * pos[None, :]
    return (jnp.cos(ang), jnp.sin(ang) * sign[None, :], jnp.cos(ang_t), jnp.sin(ang_t) * sign[:, None])


def kernel(x, positions, pool_norm, pool_w, pool_scale, kv_in_norm, w_kv_a, kv_latent_norm, w_kv_b, attn_norm, w_q_a, q_latent_norm, w_q_b, w_o, ffn_norm, w_gate, w_up, w_down, final_norm):
    batch, seq, d = x.shape
    depth = ffn_norm.shape[0]
    n_pool = pool_norm.shape[0]
    assert seq % ROW_TILE == 0 and seq % (2 * ATTN_TILE) == 0
    assert w_gate.shape[2] % FF_TILE == 0

    cos, sin, cos_t, sin_t = _rope_tables(positions)
    row = lambda v: v.reshape(1, -1)
    xs = x.reshape(batch * seq, d)

    wg, wu, wd = w_gate.astype(BF16), w_up.astype(BF16), w_down.astype(BF16)
    zpad = jnp.zeros((d, LANE - QK_ROPE_DIM), F32)
    w_pe = w_kv_a[:, KV_LORA_RANK:]
    wa_ext = jnp.concatenate([w_kv_a[:, :KV_LORA_RANK], w_pe, zpad, _swap_halves(w_pe), zpad], axis=1).astype(BF16)
    wkb = w_kv_b.reshape(KV_LORA_RANK, N_HEADS, QK_NOPE_DIM + V_DIM)
    w_kn = wkb[:, :, :QK_NOPE_DIM].reshape(KV_LORA_RANK, -1).astype(BF16)
    w_vt = wkb[:, :, QK_NOPE_DIM:].reshape(KV_LORA_RANK, -1).T.astype(BF16)

    kn = vt = kpe = None
    for l in range(depth):
        if l == n_pool:
            kn, vt, kpe = _kv_proj(xs, row(kv_in_norm), wa_ext, row(kv_latent_norm), w_kn, w_vt, cos, sin)
        if l < n_pool:
            xs = _pool_mixer(xs, seq, row(pool_norm[l]), pool_w[l].astype(BF16), row(pool_scale[l]))
        else:
            b = l - n_pool
            rank = w_q_b.shape[1]
            wqb = w_q_b[b].reshape(rank, N_HEADS, QK_DIM)
            wqb_pe = wqb[:, :, QK_NOPE_DIM:]
            wqb_ext = jnp.concatenate([
                wqb[:, :, :QK_NOPE_DIM].reshape(rank, -1),
                wqb_pe.reshape(rank, -1),
                _swap_halves(wqb_pe).reshape(rank, -1)], axis=1).T.astype(BF16)
            qn, qp = _q_proj(xs, row(attn_norm[b]), w_q_a[b].astype(BF16), row(q_latent_norm[b]), wqb_ext, cos_t, sin_t)
            attn = _attention(qn, qp, kn, kpe, vt, batch, seq)
            xs = _out_proj(xs, attn, w_o[b].astype(BF16))
        last = l == depth - 1
        xs = _ffn(xs, row(ffn_norm[l]), wg, wu, wd, row(final_norm), layer=l, final_norm=last)
    return xs.reshape(batch, seq, d)
```

```python
import functools
import math

import jax
import jax.numpy as jnp
from jax import lax
from jax.experimental import pallas as pl
from jax.experimental.pallas import tpu as pltpu

N_HEADS = 16
QK_NOPE_DIM = 128
QK_ROPE_DIM = 64
QK_DIM = QK_NOPE_DIM + QK_ROPE_DIM
V_DIM = 128
KV_LORA_RANK = 512
POOL_WINDOWS = (2, 4, 8, 16)
ROPE_BASE = 10000.0
NORM_EPS = 1e-6

LANE = 128
HALO = max(POOL_WINDOWS)
SUM_ROWS = 16
Q_PRESCALE = (1.0 / math.sqrt(QK_DIM)) * math.log2(math.e)

ROW_TILE = 512
FF_TILE = 512
FFN_ROW_TILE = 1024
FFN_SUB_ROWS = 512
ATTN_TILE = 512
VMEM_LIMIT = 56 * 1024 * 1024

F32 = jnp.float32
BF16 = jnp.bfloat16


def _rms(x, g):
    ms = jnp.mean(x * x, axis=-1, keepdims=True)
    return x * lax.rsqrt(ms + NORM_EPS) * g


def _params(semantics, flags=None):
    return pltpu.CompilerParams(dimension_semantics=semantics, vmem_limit_bytes=VMEM_LIMIT, flags=flags)


def _pool_kernel(x_ref, halo_ref, g_ref, w_ref, sc_ref, o_ref, hh_ref, *, tiles_per_seq, tm, pg):
    t = pl.program_id(0) % tiles_per_seq
    g = g_ref[...]
    hh_ref[HALO:, :] = _rms(x_ref[...], g)
    hh_ref[:HALO, :] = jnp.where(t == 0, 0.0, _rms(halo_ref[...], g))
    pos = t * tm + lax.broadcasted_iota(jnp.int32, (tm, 1), 0)
    for gi, w in enumerate(POOL_WINDOWS):
        cols = slice(gi * pg, (gi + 1) * pg)
        h = hh_ref[HALO:, cols]
        acc = h
        for j in range(1, w):
            acc = acc + hh_ref[HALO - j:HALO - j + tm, cols]
        count = jnp.minimum(pos + 1, w).astype(F32)
        diff = acc / count - h
        y = jnp.dot(diff.astype(BF16), w_ref[gi], preferred_element_type=F32)
        o_ref[:, cols] = x_ref[:, cols] + y * sc_ref[:, cols]


def _pool_mixer(x, seq, norm, w, scale):
    tokens, d = x.shape
    tm = ROW_TILE
    pg = d // len(POOL_WINDOWS)
    kern = functools.partial(_pool_kernel, tiles_per_seq=seq // tm, tm=tm, pg=pg)
    return pl.pallas_call(
        kern,
        name="pool_mixer",
        grid=(tokens // tm,),
        in_specs=[
            pl.BlockSpec((tm, d), lambda i: (i, 0)),
            pl.BlockSpec((HALO, d), lambda i: (jnp.maximum(i * (tm // HALO) - 1, 0), 0)),
            pl.BlockSpec((1, d), lambda i: (0, 0)),
            pl.BlockSpec(w.shape, lambda i: (0, 0, 0)),
            pl.BlockSpec((1, d), lambda i: (0, 0)),
        ],
        out_specs=pl.BlockSpec((tm, d), lambda i: (i, 0)),
        out_shape=jax.ShapeDtypeStruct((tokens, d), F32),
        scratch_shapes=[pltpu.VMEM((HALO + tm, d), F32)],
        compiler_params=_params(("arbitrary",)),
    )(x, x, norm, w, scale)


def _ffn_kernel(x_ref, g_ref, wg_ref, wu_ref, wd_ref, fg_ref, o_ref, hn_ref, *, final_norm):
    j = pl.program_id(1)

    halves = [slice(r * FFN_SUB_ROWS, (r + 1) * FFN_SUB_ROWS) for r in range(x_ref.shape[0] // FFN_SUB_ROWS)]

    @pl.when(j == 0)
    def _():
        for rows in halves:
            x = x_ref[rows, :]
            hn_ref[rows, :] = _rms(x, g_ref[...]).astype(BF16)
            o_ref[rows, :] = x

    for rows in halves:
        hn = hn_ref[rows, :]
        gate = jnp.dot(hn, wg_ref[...], preferred_element_type=F32)
        up = jnp.dot(hn, wu_ref[...], preferred_element_type=F32)
        act = gate / (1.0 + jnp.exp(-gate)) * up
        o_ref[rows, :] += jnp.dot(act.astype(BF16), wd_ref[...], preferred_element_type=F32)

    if final_norm:
        @pl.when(j == pl.num_programs(1) - 1)
        def _():
            for rows in halves:
                o_ref[rows, :] = _rms(o_ref[rows, :], fg_ref[...])


def _ffn(x, norm, w_gate, w_up, w_down, final_g, *, layer, final_norm):
    tokens, d = x.shape
    dff = w_gate.shape[2]
    tm, tf = FFN_ROW_TILE, FF_TILE
    kern = functools.partial(_ffn_kernel, final_norm=final_norm)
    return pl.pallas_call(
        kern,
        name="ffn_final" if final_norm else "ffn",
        grid=(tokens // tm, dff // tf),
        in_specs=[
            pl.BlockSpec((tm, d), lambda i, j: (i, 0)),
            pl.BlockSpec((1, d), lambda i, j: (0, 0)),
            pl.BlockSpec((None, d, tf), lambda i, j: (layer, 0, j)),
            pl.BlockSpec((None, d, tf), lambda i, j: (layer, 0, j)),
            pl.BlockSpec((None, tf, d), lambda i, j: (layer, j, 0)),
            pl.BlockSpec((1, d), lambda i, j: (0, 0)),
        ],
        out_specs=pl.BlockSpec((tm, d), lambda i, j: (i, 0)),
        out_shape=jax.ShapeDtypeStruct((tokens, d), F32),
        scratch_shapes=[pltpu.VMEM((tm, d), BF16)],
        compiler_params=_params(("arbitrary", "arbitrary")),
    )(x, norm, w_gate, w_up, w_down, final_g)


def _kv_kernel(x_ref, g_ref, wa_ref, gl_ref, wkn_ref, wvt_ref, cos_ref, sin_ref, kn_ref, vt_ref, kpe_ref):
    h = _rms(x_ref[...], g_ref[...]).astype(BF16)
    kv = jnp.dot(h, wa_ref[...], preferred_element_type=F32)
    c_kv = _rms(kv[:, :KV_LORA_RANK], gl_ref[...]).astype(BF16)
    pe = kv[:, KV_LORA_RANK:KV_LORA_RANK + LANE]
    pe_swapped = kv[:, KV_LORA_RANK + LANE:]
    rot = pe * cos_ref[...] + pe_swapped * sin_ref[...]
    kpe_ref[:, :LANE] = rot.astype(BF16)
    kpe_ref[:, LANE:] = pltpu.roll(rot, LANE // 2, axis=1).astype(BF16)
    kn_ref[...] = jnp.dot(c_kv, wkn_ref[...], preferred_element_type=F32).astype(BF16)
    vt = lax.dot_general(wvt_ref[...], c_kv, (((1,), (1,)), ((), ())), preferred_element_type=F32)
    vt_ref[0] = vt.astype(BF16)


def _kv_proj(x, norm, wa_ext, latent_norm, w_kn, w_vt, cos, sin):
    tokens, d = x.shape
    tm = ATTN_TILE
    n_kn, n_v = w_kn.shape[1], w_vt.shape[0]
    return pl.pallas_call(
        _kv_kernel,
        name="kv_proj",
        grid=(tokens // tm,),
        in_specs=[
            pl.BlockSpec((tm, d), lambda i: (i, 0)),
            pl.BlockSpec((1, d), lambda i: (0, 0)),
            pl.BlockSpec(wa_ext.shape, lambda i: (0, 0)),
            pl.BlockSpec((1, KV_LORA_RANK), lambda i: (0, 0)),
            pl.BlockSpec(w_kn.shape, lambda i: (0, 0)),
            pl.BlockSpec(w_vt.shape, lambda i: (0, 0)),
            pl.BlockSpec((tm, LANE), lambda i: (i, 0)),
            pl.BlockSpec((tm, LANE), lambda i: (i, 0)),
        ],
        out_specs=[
            pl.BlockSpec((tm, n_kn), lambda i: (i, 0)),
            pl.BlockSpec((1, n_v, tm), lambda i: (i, 0, 0)),
            pl.BlockSpec((tm, 2 * LANE), lambda i: (i, 0)),
        ],
        out_shape=[
            jax.ShapeDtypeStruct((tokens, n_kn), BF16),
            jax.ShapeDtypeStruct((tokens // tm, n_v, tm), BF16),
            jax.ShapeDtypeStruct((tokens, 2 * LANE), BF16),
        ],
        compiler_params=_params(("arbitrary",)),
    )(x, norm, wa_ext, latent_norm, w_kn, w_vt, cos, sin)


def _q_kernel(x_ref, g_ref, wa_ref, gl_ref, wb_ref, cos_ref, sin_ref, qn_ref, qp_ref, *, n_nope, n_pe):
    h = _rms(x_ref[...], g_ref[...]).astype(BF16)
    cq = jnp.dot(h, wa_ref[...], preferred_element_type=F32)
    cqn = _rms(cq, gl_ref[...]).astype(BF16)
    qt = lax.dot_general(wb_ref[...], cqn, (((1,), (1,)), ((), ())), preferred_element_type=F32)
    qn_ref[...] = (qt[:n_nope, :] * Q_PRESCALE).astype(BF16)
    cos = cos_ref[...] * Q_PRESCALE
    sin = sin_ref[...] * Q_PRESCALE
    for p in range(n_pe // LANE):
        pe = qt[n_nope + p * LANE:n_nope + (p + 1) * LANE, :]
        sw = qt[n_nope + n_pe + p * LANE:n_nope + n_pe + (p + 1) * LANE, :]
        qp_ref[p * LANE:(p + 1) * LANE, :] = (pe * cos + sw * sin).astype(BF16)


def _q_proj(x, norm, wa, latent_norm, wb_ext, cos, sin):
    tokens, d = x.shape
    tm = ROW_TILE
    n_nope = N_HEADS * QK_NOPE_DIM
    n_pe = N_HEADS * QK_ROPE_DIM
    kern = functools.partial(_q_kernel, n_nope=n_nope, n_pe=n_pe)
    return pl.pallas_call(
        kern,
        name="q_proj",
        grid=(tokens // tm,),
        in_specs=[
            pl.BlockSpec((tm, d), lambda i: (i, 0)),
            pl.BlockSpec((1, d), lambda i: (0, 0)),
            pl.BlockSpec(wa.shape, lambda i: (0, 0)),
            pl.BlockSpec((1, wa.shape[1]), lambda i: (0, 0)),
            pl.BlockSpec(wb_ext.shape, lambda i: (0, 0)),
            pl.BlockSpec((LANE, tm), lambda i: (0, i)),
            pl.BlockSpec((LANE, tm), lambda i: (0, i)),
        ],
        out_specs=[
            pl.BlockSpec((n_nope, tm), lambda i: (0, i)),
            pl.BlockSpec((n_pe, tm), lambda i: (0, i)),
        ],
        out_shape=[
            jax.ShapeDtypeStruct((n_nope, tokens), BF16),
            jax.ShapeDtypeStruct((n_pe, tokens), BF16),
        ],
        compiler_params=_params(("arbitrary",)),
    )(x, norm, wa, latent_norm, wb_ext, cos, sin)


def _attn_kernel(qn_ref, qp_ref, kn_ref, kp_ref, vt_ref, o_ref, st_ref, acc_ref, *, blk):
    qi = pl.program_id(2)
    d0 = 2 * qi
    qp = qp_ref[...]
    qts = [jnp.concatenate([qn_ref[h * LANE:(h + 1) * LANE, :], qp], axis=0) for h in range(2)]
    acc_ref[...] = jnp.zeros(acc_ref.shape, F32)

    def scores(h, j, qt):
        off = pl.multiple_of(j * blk, blk)
        k = jnp.concatenate([kn_ref[pl.ds(off, blk), h * LANE:(h + 1) * LANE],
                             kp_ref[pl.ds(off, blk), h * LANE:(h + 1) * LANE]], axis=1)
        return jnp.dot(k, qt, preferred_element_type=F32)

    ones_rows = jnp.ones((SUM_ROWS, blk), BF16)

    def consume(h, s, slot, vblk, m, cmax):
        m_new = jnp.maximum(m, cmax)
        alpha = jnp.exp2(m - m_new)
        pt = jnp.exp2(st_ref[h, s, slot] - m_new)
        vt = jnp.concatenate([vt_ref[vblk, h * V_DIM:(h + 1) * V_DIM, :], ones_rows], axis=0)
        acc_ref[h, s] = alpha * acc_ref[h, s] + jnp.dot(vt, pt.astype(BF16), preferred_element_type=F32)
        return m_new

    def colmax(st):
        return jnp.max(st, axis=0, keepdims=True)

    def step(t, carry, slot_a):
        vblk = jnp.where(t == 0, d0, t - 1)
        out = []
        for h in range(2):
            st = scores(h, t, qts[h])
            for s in range(2):
                slot = slot_a if s == 0 else 1 - slot_a
                m, cmax = carry[2 * h + s]
                st_s = st[:, s * blk:(s + 1) * blk]
                st_ref[h, s, 1 - slot] = st_s
                m = consume(h, s, slot, vblk, m, cmax)
                out.append((m, colmax(st_s)))
        return tuple(out)

    key = lax.broadcasted_iota(jnp.int32, (blk, blk), 0)
    qry = lax.broadcasted_iota(jnp.int32, (blk, blk), 1)
    causal = key <= qry
    m0 = jnp.full((1, blk), -jnp.inf, F32)
    carry = []
    for h in range(2):
        s_d0 = scores(h, d0, qts[h])
        diag_a = jnp.where(causal, s_d0[:, :blk], -jnp.inf)
        full_b = s_d0[:, blk:]
        diag_b = jnp.where(causal, scores(h, d0 + 1, qts[h][:, blk:]), -jnp.inf)
        st_ref[h, 0, 0] = diag_a
        st_ref[h, 1, 0] = diag_b
        st_ref[h, 1, 1] = full_b
        carry.append((m0, colmax(diag_a)))
        m = consume(h, 1, 0, d0 + 1, m0, colmax(diag_b))
        carry.append((m, colmax(full_b)))

    def pair(u, carry):
        return step(2 * u + 1, step(2 * u, carry, 0), 1)

    carry = lax.fori_loop(0, qi, pair, tuple(carry))

    vblk = jnp.maximum(d0 - 1, 0)
    for h in range(2):
        for s in range(2):
            consume(h, s, s, vblk, *carry[2 * h + s])
            out = acc_ref[h, s, :V_DIM, :] / acc_ref[h, s, V_DIM:V_DIM + 1, :]
            o_ref[s * blk:(s + 1) * blk, h * LANE:(h + 1) * LANE] = out.T.astype(BF16)


def _attention(qn, qp, kn, kpe, vt, batch, seq):
    tokens = qn.shape[1]
    blk = ATTN_TILE
    nk = seq // blk
    nq = seq // (2 * blk)
    kern = functools.partial(_attn_kernel, blk=blk)
    return pl.pallas_call(
        kern,
        name="attention",
        grid=(batch, N_HEADS // 2, nq),
        in_specs=[
            pl.BlockSpec((2 * LANE, 2 * blk), lambda b, p, i: (p, b * nq + i)),
            pl.BlockSpec((LANE, 2 * blk), lambda b, p, i: (p, b * nq + i)),
            pl.BlockSpec((seq, 2 * LANE), lambda b, p, i: (b, p)),
            pl.BlockSpec((seq, 2 * LANE), lambda b, p, i: (b, 0)),
            pl.BlockSpec((nk, 2 * V_DIM, blk), lambda b, p, i: (b, p, 0)),
        ],
        out_specs=pl.BlockSpec((2 * blk, 2 * LANE), lambda b, p, i: (b * nq + i, p)),
        out_shape=jax.ShapeDtypeStruct((tokens, N_HEADS * V_DIM), BF16),
        scratch_shapes=[pltpu.VMEM((2, 2, 2, blk, blk), F32), pltpu.VMEM((2, 2, V_DIM + SUM_ROWS, blk), F32)],
        compiler_params=_params(("arbitrary", "arbitrary", "arbitrary")),
    )(qn, qp, kn, kpe, vt)


def _out_proj_kernel(x_ref, a_ref, w_ref, o_ref):
    o_ref[...] = x_ref[...] + jnp.dot(a_ref[...], w_ref[...], preferred_element_type=F32)


def _out_proj(x, attn, w_o):
    tokens, d = x.shape
    tm = ROW_TILE
    return pl.pallas_call(
        _out_proj_kernel,
        name="out_proj",
        grid=(tokens // tm,),
        in_specs=[
            pl.BlockSpec((tm, d), lambda i: (i, 0)),
            pl.BlockSpec((tm, attn.shape[1]), lambda i: (i, 0)),
            pl.BlockSpec(w_o.shape, lambda i: (0, 0)),
        ],
        out_specs=pl.BlockSpec((tm, d), lambda i: (i, 0)),
        out_shape=jax.ShapeDtypeStruct((tokens, d), F32),
        compiler_params=_params(("arbitrary",)),
    )(x, attn, w_o)


def _swap_halves(w):
    half = w.shape[-1] // 2
    return jnp.concatenate([w[..., half:], w[..., :half]], axis=-1)


def _rope_tables(positions):
    half = QK_ROPE_DIM // 2
    inv_freq = jnp.tile(ROPE_BASE ** (-jnp.arange(half, dtype=F32) / half), LANE // half)
    sign = jnp.tile(jnp.concatenate([-jnp.ones((half,), F32), jnp.ones((half,), F32)]), LANE // QK_ROPE_DIM)
    pos = positions.astype(F32).reshape(-1)
    ang = pos[:, None] * inv_freq[None, :]
    ang_t = inv_freq[:, None] * pos[None, :]
    return (jnp.cos(ang), jnp.sin(ang) * sign[None, :], jnp.cos(ang_t), jnp.sin(ang_t) * sign[:, None])


def kernel(x, positions, pool_norm, pool_w, pool_scale, kv_in_norm, w_kv_a, kv_latent_norm, w_kv_b, attn_norm, w_q_a, q_latent_norm, w_q_b, w_o, ffn_norm, w_gate, w_up, w_down, final_norm):
    batch, seq, d = x.shape
    depth = ffn_norm.shape[0]
    n_pool = pool_norm.shape[0]
    assert seq % ROW_TILE == 0 and seq % (2 * ATTN_TILE) == 0
    assert w_gate.shape[2] % FF_TILE == 0

    cos, sin, cos_t, sin_t = _rope_tables(positions)
    row = lambda v: v.reshape(1, -1)
    xs = x.reshape(batch * seq, d)

    wg, wu, wd = w_gate.astype(BF16), w_up.astype(BF16), w_down.astype(BF16)
    zpad = jnp.zeros((d, LANE - QK_ROPE_DIM), F32)
    w_pe = w_kv_a[:, KV_LORA_RANK:]
    wa_ext = jnp.concatenate([w_kv_a[:, :KV_LORA_RANK], w_pe, zpad, _swap_halves(w_pe), zpad], axis=1).astype(BF16)
    wkb = w_kv_b.reshape(KV_LORA_RANK, N_HEADS, QK_NOPE_DIM + V_DIM)
    w_kn = wkb[:, :, :QK_NOPE_DIM].reshape(KV_LORA_RANK, -1).astype(BF16)
    w_vt = wkb[:, :, QK_NOPE_DIM:].reshape(KV_LORA_RANK, -1).T.astype(BF16)

    kn = vt = kpe = None
    for l in range(depth):
        if l == n_pool:
            kn, vt, kpe = _kv_proj(xs, row(kv_in_norm), wa_ext, row(kv_latent_norm), w_kn, w_vt, cos, sin)
        if l < n_pool:
            xs = _pool_mixer(xs, seq, row(pool_norm[l]), pool_w[l].astype(BF16), row(pool_scale[l]))
        else:
            b = l - n_pool
            rank = w_q_b.shape[1]
            wqb = w_q_b[b].reshape(rank, N_HEADS, QK_DIM)
            wqb_pe = wqb[:, :, QK_NOPE_DIM:]
            wqb_ext = jnp.concatenate([
                wqb[:, :, :QK_NOPE_DIM].reshape(rank, -1),
                wqb_pe.reshape(rank, -1),
                _swap_halves(wqb_pe).reshape(rank, -1)], axis=1).T.astype(BF16)
            qn, qp = _q_proj(xs, row(attn_norm[b]), w_q_a[b].astype(BF16), row(q_latent_norm[b]), wqb_ext, cos_t, sin_t)
            attn = _attention(qn, qp, kn, kpe, vt, batch, seq)
            xs = _out_proj(xs, attn, w_o[b].astype(BF16))
        last = l == depth - 1
        xs = _ffn(xs, row(ffn_norm[l]), wg, wu, wd, row(final_norm), layer=l, final_norm=last)
    return xs.reshape(batch, seq, d)
```

```python
import functools
import math

import jax
import jax.numpy as jnp
from jax import lax
from jax.experimental import pallas as pl
from jax.experimental.pallas import tpu as pltpu

N_HEADS = 16
QK_NOPE_DIM = 128
QK_ROPE_DIM = 64
QK_DIM = QK_NOPE_DIM + QK_ROPE_DIM
V_DIM = 128
KV_LORA_RANK = 512
POOL_WINDOWS = (2, 4, 8, 16)
ROPE_BASE = 10000.0
NORM_EPS = 1e-6

LANE = 128
HALO = max(POOL_WINDOWS)
SUM_ROWS = 16
Q_PRESCALE = (1.0 / math.sqrt(QK_DIM)) * math.log2(math.e)

ROW_TILE = 512
FF_TILE = 512
FFN_ROW_TILE = 1024
FFN_SUB_ROWS = 512
ATTN_TILE = 512
VMEM_LIMIT = 56 * 1024 * 1024

F32 = jnp.float32
BF16 = jnp.bfloat16


def _rms(x, g):
    ms = jnp.mean(x * x, axis=-1, keepdims=True)
    return x * lax.rsqrt(ms + NORM_EPS) * g


def _params(semantics, flags=None):
    return pltpu.CompilerParams(dimension_semantics=semantics, vmem_limit_bytes=VMEM_LIMIT, flags=flags)


def _pool_kernel(x_ref, halo_ref, g_ref, w_ref, sc_ref, o_ref, hh_ref, *, tiles_per_seq, tm, pg):
    t = pl.program_id(0) % tiles_per_seq
    g = g_ref[...]
    hh_ref[HALO:, :] = _rms(x_ref[...], g)
    hh_ref[:HALO, :] = jnp.where(t == 0, 0.0, _rms(halo_ref[...], g))
    pos = t * tm + lax.broadcasted_iota(jnp.int32, (tm, 1), 0)
    for gi, w in enumerate(POOL_WINDOWS):
        cols = slice(gi * pg, (gi + 1) * pg)
        h = hh_ref[HALO:, cols]
        acc = h
        for j in range(1, w):
            acc = acc + hh_ref[HALO - j:HALO - j + tm, cols]
        count = jnp.minimum(pos + 1, w).astype(F32)
        diff = acc / count - h
        y = jnp.dot(diff.astype(BF16), w_ref[gi], preferred_element_type=F32)
        o_ref[:, cols] = x_ref[:, cols] + y * sc_ref[:, cols]


def _pool_mixer(x, seq, norm, w, scale):
    tokens, d = x.shape
    tm = ROW_TILE
    pg = d // len(POOL_WINDOWS)
    kern = functools.partial(_pool_kernel, tiles_per_seq=seq // tm, tm=tm, pg=pg)
    return pl.pallas_call(
        kern,
        name="pool_mixer",
        grid=(tokens // tm,),
        in_specs=[
            pl.BlockSpec((tm, d), lambda i: (i, 0)),
            pl.BlockSpec((HALO, d), lambda i: (jnp.maximum(i * (tm // HALO) - 1, 0), 0)),
            pl.BlockSpec((1, d), lambda i: (0, 0)),
            pl.BlockSpec(w.shape, lambda i: (0, 0, 0)),
            pl.BlockSpec((1, d), lambda i: (0, 0)),
        ],
        out_specs=pl.BlockSpec((tm, d), lambda i: (i, 0)),
        out_shape=jax.ShapeDtypeStruct((tokens, d), F32),
        scratch_shapes=[pltpu.VMEM((HALO + tm, d), F32)],
        compiler_params=_params(("arbitrary",)),
    )(x, x, norm, w, scale)


def _ffn_kernel(x_ref, g_ref, wg_ref, wu_ref, wd_ref, fg_ref, o_ref, hn_ref, *, final_norm):
    j = pl.program_id(1)

    halves = [slice(r * FFN_SUB_ROWS, (r + 1) * FFN_SUB_ROWS) for r in range(x_ref.shape[0] // FFN_SUB_ROWS)]

    def chunk(first):
        for rows in halves:
            if first:
                hn = _rms(x_ref[rows, :], g_ref[...]).astype(BF16)
                hn_ref[rows, :] = hn
            else:
                hn = hn_ref[rows, :]
            gate = jnp.dot(hn, wg_ref[...], preferred_element_type=F32)
            up = jnp.dot(hn, wu_ref[...], preferred_element_type=F32)
            act = gate / (1.0 + jnp.exp(-gate)) * up
            down = jnp.dot(act.astype(BF16), wd_ref[...], preferred_element_type=F32)
            o_ref[rows, :] = (x_ref[rows, :] if first else o_ref[rows, :]) + down

    pl.when(j == 0)(functools.partial(chunk, True))
    pl.when(j > 0)(functools.partial(chunk, False))

    if final_norm:
        @pl.when(j == pl.num_programs(1) - 1)
        def _():
            for rows in halves:
                o_ref[rows, :] = _rms(o_ref[rows, :], fg_ref[...])


def _ffn(x, norm, w_gate, w_up, w_down, final_g, *, layer, final_norm):
    tokens, d = x.shape
    dff = w_gate.shape[2]
    tm, tf = FFN_ROW_TILE, FF_TILE
    kern = functools.partial(_ffn_kernel, final_norm=final_norm)
    return pl.pallas_call(
        kern,
        name="ffn_final" if final_norm else "ffn",
        grid=(tokens // tm, dff // tf),
        in_specs=[
            pl.BlockSpec((tm, d), lambda i, j: (i, 0)),
            pl.BlockSpec((1, d), lambda i, j: (0, 0)),
            pl.BlockSpec((None, d, tf), lambda i, j: (layer, 0, j)),
            pl.BlockSpec((None, d, tf), lambda i, j: (layer, 0, j)),
            pl.BlockSpec((None, tf, d), lambda i, j: (layer, j, 0)),
            pl.BlockSpec((1, d), lambda i, j: (0, 0)),
        ],
        out_specs=pl.BlockSpec((tm, d), lambda i, j: (i, 0)),
        out_shape=jax.ShapeDtypeStruct((tokens, d), F32),
        scratch_shapes=[pltpu.VMEM((tm, d), BF16)],
        compiler_params=_params(("arbitrary", "arbitrary")),
    )(x, norm, w_gate, w_up, w_down, final_g)


def _kv_kernel(x_ref, g_ref, wa_ref, gl_ref, wkn_ref, wvt_ref, cos_ref, sin_ref, kn_ref, vt_ref, kpe_ref):
    h = _rms(x_ref[...], g_ref[...]).astype(BF16)
    kv = jnp.dot(h, wa_ref[...], preferred_element_type=F32)
    c_kv = _rms(kv[:, :KV_LORA_RANK], gl_ref[...]).astype(BF16)
    pe = kv[:, KV_LORA_RANK:KV_LORA_RANK + LANE]
    pe_swapped = kv[:, KV_LORA_RANK + LANE:]
    rot = pe * cos_ref[...] + pe_swapped * sin_ref[...]
    kpe_ref[:, :LANE] = rot.astype(BF16)
    kpe_ref[:, LANE:] = pltpu.roll(rot, LANE // 2, axis=1).astype(BF16)
    kn_ref[...] = jnp.dot(c_kv, wkn_ref[...], preferred_element_type=F32).astype(BF16)
    vt = lax.dot_general(wvt_ref[...], c_kv, (((1,), (1,)), ((), ())), preferred_element_type=F32)
    vt_ref[0] = vt.astype(BF16)


def _kv_proj(x, norm, wa_ext, latent_norm, w_kn, w_vt, cos, sin):
    tokens, d = x.shape
    tm = ATTN_TILE
    n_kn, n_v = w_kn.shape[1], w_vt.shape[0]
    return pl.pallas_call(
        _kv_kernel,
        name="kv_proj",
        grid=(tokens // tm,),
        in_specs=[
            pl.BlockSpec((tm, d), lambda i: (i, 0)),
            pl.BlockSpec((1, d), lambda i: (0, 0)),
            pl.BlockSpec(wa_ext.shape, lambda i: (0, 0)),
            pl.BlockSpec((1, KV_LORA_RANK), lambda i: (0, 0)),
            pl.BlockSpec(w_kn.shape, lambda i: (0, 0)),
            pl.BlockSpec(w_vt.shape, lambda i: (0, 0)),
            pl.BlockSpec((tm, LANE), lambda i: (i, 0)),
            pl.BlockSpec((tm, LANE), lambda i: (i, 0)),
        ],
        out_specs=[
            pl.BlockSpec((tm, n_kn), lambda i: (i, 0)),
            pl.BlockSpec((1, n_v, tm), lambda i: (i, 0, 0)),
            pl.BlockSpec((tm, 2 * LANE), lambda i: (i, 0)),
        ],
        out_shape=[
            jax.ShapeDtypeStruct((tokens, n_kn), BF16),
            jax.ShapeDtypeStruct((tokens // tm, n_v, tm), BF16),
            jax.ShapeDtypeStruct((tokens, 2 * LANE), BF16),
        ],
        compiler_params=_params(("arbitrary",)),
    )(x, norm, wa_ext, latent_norm, w_kn, w_vt, cos, sin)


def _q_kernel(x_ref, g_ref, wa_ref, gl_ref, wb_ref, cos_ref, sin_ref, qn_ref, qp_ref, *, n_nope, n_pe):
    h = _rms(x_ref[...], g_ref[...]).astype(BF16)
    cq = jnp.dot(h, wa_ref[...], preferred_element_type=F32)
    cqn = _rms(cq, gl_ref[...]).astype(BF16)
    qt = lax.dot_general(wb_ref[...], cqn, (((1,), (1,)), ((), ())), preferred_element_type=F32)
    qn_ref[...] = (qt[:n_nope, :] * Q_PRESCALE).astype(BF16)
    cos = cos_ref[...] * Q_PRESCALE
    sin = sin_ref[...] * Q_PRESCALE
    for p in range(n_pe // LANE):
        pe = qt[n_nope + p * LANE:n_nope + (p + 1) * LANE, :]
        sw = qt[n_nope + n_pe + p * LANE:n_nope + n_pe + (p + 1) * LANE, :]
        qp_ref[p * LANE:(p + 1) * LANE, :] = (pe * cos + sw * sin).astype(BF16)


def _q_proj(x, norm, wa, latent_norm, wb_ext, cos, sin):
    tokens, d = x.shape
    tm = ROW_TILE
    n_nope = N_HEADS * QK_NOPE_DIM
    n_pe = N_HEADS * QK_ROPE_DIM
    kern = functools.partial(_q_kernel, n_nope=n_nope, n_pe=n_pe)
    return pl.pallas_call(
        kern,
        name="q_proj",
        grid=(tokens // tm,),
        in_specs=[
            pl.BlockSpec((tm, d), lambda i: (i, 0)),
            pl.BlockSpec((1, d), lambda i: (0, 0)),
            pl.BlockSpec(wa.shape, lambda i: (0, 0)),
            pl.BlockSpec((1, wa.shape[1]), lambda i: (0, 0)),
            pl.BlockSpec(wb_ext.shape, lambda i: (0, 0)),
            pl.BlockSpec((LANE, tm), lambda i: (0, i)),
            pl.BlockSpec((LANE, tm), lambda i: (0, i)),
        ],
        out_specs=[
            pl.BlockSpec((n_nope, tm), lambda i: (0, i)),
            pl.BlockSpec((n_pe, tm), lambda i: (0, i)),
        ],
        out_shape=[
            jax.ShapeDtypeStruct((n_nope, tokens), BF16),
            jax.ShapeDtypeStruct((n_pe, tokens), BF16),
        ],
        compiler_params=_params(("arbitrary",)),
    )(x, norm, wa, latent_norm, wb_ext, cos, sin)


def _attn_kernel(qn_ref, qp_ref, kn_ref, kp_ref, vt_ref, o_ref, st_ref, acc_ref, *, blk):
    qi = pl.program_id(2)
    d0 = 2 * qi
    qp = qp_ref[...]
    qts = [jnp.concatenate([qn_ref[h * LANE:(h + 1) * LANE, :], qp], axis=0) for h in range(2)]
    acc_ref[...] = jnp.zeros(acc_ref.shape, F32)

    def scores(h, j, qt):
        off = pl.multiple_of(j * blk, blk)
        k = jnp.concatenate([kn_ref[pl.ds(off, blk), h * LANE:(h + 1) * LANE],
                             kp_ref[pl.ds(off, blk), h * LANE:(h + 1) * LANE]], axis=1)
        return jnp.dot(k, qt, preferred_element_type=F32)

    ones_rows = jnp.ones((SUM_ROWS, blk), BF16)

    def consume(h, s, slot, vblk, m, cmax):
        m_new = jnp.maximum(m, cmax)
        alpha = jnp.exp2(m - m_new)
        pt = jnp.exp2(st_ref[h, s, slot] - m_new)
        vt = jnp.concatenate([vt_ref[vblk, h * V_DIM:(h + 1) * V_DIM, :], ones_rows], axis=0)
        acc_ref[h, s] = alpha * acc_ref[h, s] + jnp.dot(vt, pt.astype(BF16), preferred_element_type=F32)
        return m_new

    def colmax(st):
        return jnp.max(st, axis=0, keepdims=True)

    def step(t, carry, slot, mask_a=None):
        out = []
        for h in range(2):
            st = scores(h, t, qts[h])
            for s in range(2):
                m, cmax = carry[2 * h + s]
                st_s = st[:, s * blk:(s + 1) * blk]
                if s == 0 and mask_a is not None:
                    st_s = jnp.where(mask_a, st_s, -jnp.inf)
                st_ref[h, s, 1 - slot] = st_s
                m = consume(h, s, slot, t - 1, m, cmax)
                out.append((m, colmax(st_s)))
        return tuple(out)

    key = lax.broadcasted_iota(jnp.int32, (blk, blk), 0)
    qry = lax.broadcasted_iota(jnp.int32, (blk, blk), 1)
    causal = key <= qry
    m0 = jnp.full((1, blk), -jnp.inf, F32)

    first_mask = jnp.logical_or(causal, qi > 0)
    carry = []
    for h in range(2):
        st = scores(h, 0, qts[h])
        for s in range(2):
            st_s = st[:, s * blk:(s + 1) * blk]
            if s == 0:
                st_s = jnp.where(first_mask, st_s, -jnp.inf)
            st_ref[h, s, 0] = st_s
            carry.append((m0, colmax(st_s)))

    def pair(u, carry):
        return step(2 * u + 2, step(2 * u + 1, carry, 0), 1)

    carry = lax.fori_loop(0, jnp.maximum(qi - 1, 0), pair, tuple(carry))

    def tail(carry):
        return step(d0, step(d0 - 1, carry, 0), 1, mask_a=causal)

    carry = lax.cond(qi >= 1, tail, lambda c: c, carry)

    for h in range(2):
        st_b = jnp.where(causal, scores(h, d0 + 1, qts[h][:, blk:]), -jnp.inf)
        st_ref[h, 1, 1] = st_b
        ms = [consume(h, s, 0, d0, *carry[2 * h + s]) for s in range(2)]
        consume(h, 1, 1, d0 + 1, ms[1], colmax(st_b))
        for s in range(2):
            out = acc_ref[h, s, :V_DIM, :] / acc_ref[h, s, V_DIM:V_DIM + 1, :]
            o_ref[s * blk:(s + 1) * blk, h * LANE:(h + 1) * LANE] = out.T.astype(BF16)


def _attention(qn, qp, kn, kpe, vt, batch, seq):
    tokens = qn.shape[1]
    blk = ATTN_TILE
    nk = seq // blk
    nq = seq // (2 * blk)
    kern = functools.partial(_attn_kernel, blk=blk)
    return pl.pallas_call(
        kern,
        name="attention",
        grid=(batch, N_HEADS // 2, nq),
        in_specs=[
            pl.BlockSpec((2 * LANE, 2 * blk), lambda b, p, i: (p, b * nq + i)),
            pl.BlockSpec((LANE, 2 * blk), lambda b, p, i: (p, b * nq + i)),
            pl.BlockSpec((seq, 2 * LANE), lambda b, p, i: (b, p)),
            pl.BlockSpec((seq, 2 * LANE), lambda b, p, i: (b, 0)),
            pl.BlockSpec((nk, 2 * V_DIM, blk), lambda b, p, i: (b, p, 0)),
        ],
        out_specs=pl.BlockSpec((2 * blk, 2 * LANE), lambda b, p, i: (b * nq + i, p)),
        out_shape=jax.ShapeDtypeStruct((tokens, N_HEADS * V_DIM), BF16),
        scratch_shapes=[pltpu.VMEM((2, 2, 2, blk, blk), F32), pltpu.VMEM((2, 2, V_DIM + SUM_ROWS, blk), F32)],
        compiler_params=_params(("arbitrary", "arbitrary", "arbitrary")),
    )(qn, qp, kn, kpe, vt)


def _out_proj_kernel(x_ref, a_ref, w_ref, o_ref):
    o_ref[...] = x_ref[...] + jnp.dot(a_ref[...], w_ref[...], preferred_element_type=F32)


def _out_proj(x, attn, w_o):
    tokens, d = x.shape
    tm = ROW_TILE
    return pl.pallas_call(
        _out_proj_kernel,
        name="out_proj",
        grid=(tokens // tm,),
        in_specs=[
            pl.BlockSpec((tm, d), lambda i: (i, 0)),
            pl.BlockSpec((tm, attn.shape[1]), lambda i: (i, 0)),
            pl.BlockSpec(w_o.shape, lambda i: (0, 0)),
        ],
        out_specs=pl.BlockSpec((tm, d), lambda i: (i, 0)),
        out_shape=jax.ShapeDtypeStruct((tokens, d), F32),
        compiler_params=_params(("arbitrary",)),
    )(x, attn, w_o)


def _swap_halves(w):
    half = w.shape[-1] // 2
    return jnp.concatenate([w[..., half:], w[..., :half]], axis=-1)


def _rope_tables(positions):
    half = QK_ROPE_DIM // 2
    inv_freq = jnp.tile(ROPE_BASE ** (-jnp.arange(half, dtype=F32) / half), LANE // half)
    sign = jnp.tile(jnp.concatenate([-jnp.ones((half,), F32), jnp.ones((half,), F32)]), LANE // QK_ROPE_DIM)
    pos = positions.astype(F32).reshape(-1)
    ang = pos[:, None] * inv_freq[None, :]
    ang_t = inv_freq[:, None] * pos[None, :]
    return (jnp.cos(ang), jnp.sin(ang) * sign[None, :], jnp.cos(ang_t), jnp.sin(ang_t) * sign[:, None])


def kernel(x, positions, pool_norm, pool_w, pool_scale, kv_in_norm, w_kv_a, kv_latent_norm, w_kv_b, attn_norm, w_q_a, q_latent_norm, w_q_b, w_o, ffn_norm, w_gate, w_up, w_down, final_norm):
    batch, seq, d = x.shape
    depth = ffn_norm.shape[0]
    n_pool = pool_norm.shape[0]
    assert seq % ROW_TILE == 0 and seq % (2 * ATTN_TILE) == 0
    assert w_gate.shape[2] % FF_TILE == 0

    cos, sin, cos_t, sin_t = _rope_tables(positions)
    row = lambda v: v.reshape(1, -1)
    xs = x.reshape(batch * seq, d)

    wg, wu, wd = w_gate.astype(BF16), w_up.astype(BF16), w_down.astype(BF16)
    zpad = jnp.zeros((d, LANE - QK_ROPE_DIM), F32)
    w_pe = w_kv_a[:, KV_LORA_RANK:]
    wa_ext = jnp.concatenate([w_kv_a[:, :KV_LORA_RANK], w_pe, zpad, _swap_halves(w_pe), zpad], axis=1).astype(BF16)
    wkb = w_kv_b.reshape(KV_LORA_RANK, N_HEADS, QK_NOPE_DIM + V_DIM)
    w_kn = wkb[:, :, :QK_NOPE_DIM].reshape(KV_LORA_RANK, -1).astype(BF16)
    w_vt = wkb[:, :, QK_NOPE_DIM:].reshape(KV_LORA_RANK, -1).T.astype(BF16)

    kn = vt = kpe = None
    for l in range(depth):
        if l == n_pool:
            kn, vt, kpe = _kv_proj(xs, row(kv_in_norm), wa_ext, row(kv_latent_norm), w_kn, w_vt, cos, sin)
        if l < n_pool:
            xs = _pool_mixer(xs, seq, row(pool_norm[l]), pool_w[l].astype(BF16), row(pool_scale[l]))
        else:
            b = l - n_pool
            rank = w_q_b.shape[1]
            wqb = w_q_b[b].reshape(rank, N_HEADS, QK_DIM)
            wqb_pe = wqb[:, :, QK_NOPE_DIM:]
            wqb_ext = jnp.concatenate([
                wqb[:, :, :QK_NOPE_DIM].reshape(rank, -1),
                wqb_pe.reshape(rank, -1),
                _swap_halves(wqb_pe).reshape(rank, -1)], axis=1).T.astype(BF16)
            qn, qp = _q_proj(xs, row(attn_norm[b]), w_q_a[b].astype(BF16), row(q_latent_norm[b]), wqb_ext, cos_t, sin_t)
            attn = _attention(qn, qp, kn, kpe, vt, batch, seq)
            xs = _out_proj(xs, attn, w_o[b].astype(BF16))
        last = l == depth - 1
        xs = _ffn(xs, row(ffn_norm[l]), wg, wu, wd, row(final_norm), layer=l, final_norm=last)
    return xs.reshape(batch, seq, d)
```

```python
import functools
import math

import jax
import jax.numpy as jnp
from jax import lax
from jax.experimental import pallas as pl
from jax.experimental.pallas import tpu as pltpu

N_HEADS = 16
QK_NOPE_DIM = 128
QK_ROPE_DIM = 64
QK_DIM = QK_NOPE_DIM + QK_ROPE_DIM
V_DIM = 128
KV_LORA_RANK = 512
POOL_WINDOWS = (2, 4, 8, 16)
ROPE_BASE = 10000.0
NORM_EPS = 1e-6

LANE = 128
HALO = max(POOL_WINDOWS)
SUM_ROWS = 16
Q_PRESCALE = (1.0 / math.sqrt(QK_DIM)) * math.log2(math.e)

ROW_TILE = 512
FF_TILE = 512
FFN_ROW_TILE = 1024
FFN_SUB_ROWS = 512
ATTN_TILE = 512
ATTN_SUB_TILES = 2
VMEM_LIMIT = 56 * 1024 * 1024

F32 = jnp.float32
BF16 = jnp.bfloat16


def _rms(x, g):
    ms = jnp.mean(x * x, axis=-1, keepdims=True)
    return x * lax.rsqrt(ms + NORM_EPS) * g


def _params(semantics, flags=None):
    return pltpu.CompilerParams(dimension_semantics=semantics, vmem_limit_bytes=VMEM_LIMIT, flags=flags)


def _pool_kernel(x_ref, halo_ref, g_ref, w_ref, sc_ref, o_ref, hh_ref, *, tiles_per_seq, tm, pg):
    t = pl.program_id(0) % tiles_per_seq
    g = g_ref[...]
    hh_ref[HALO:, :] = _rms(x_ref[...], g)
    hh_ref[:HALO, :] = jnp.where(t == 0, 0.0, _rms(halo_ref[...], g))
    pos = t * tm + lax.broadcasted_iota(jnp.int32, (tm, 1), 0)
    for gi, w in enumerate(POOL_WINDOWS):
        cols = slice(gi * pg, (gi + 1) * pg)
        h = hh_ref[HALO:, cols]
        acc = h
        for j in range(1, w):
            acc = acc + hh_ref[HALO - j:HALO - j + tm, cols]
        count = jnp.minimum(pos + 1, w).astype(F32)
        diff = acc / count - h
        y = jnp.dot(diff.astype(BF16), w_ref[gi], preferred_element_type=F32)
        o_ref[:, cols] = x_ref[:, cols] + y * sc_ref[:, cols]


def _pool_mixer(x, seq, norm, w, scale):
    tokens, d = x.shape
    tm = ROW_TILE
    pg = d // len(POOL_WINDOWS)
    kern = functools.partial(_pool_kernel, tiles_per_seq=seq // tm, tm=tm, pg=pg)
    return pl.pallas_call(
        kern,
        name="pool_mixer",
        grid=(tokens // tm,),
        in_specs=[
            pl.BlockSpec((tm, d), lambda i: (i, 0)),
            pl.BlockSpec((HALO, d), lambda i: (jnp.maximum(i * (tm // HALO) - 1, 0), 0)),
            pl.BlockSpec((1, d), lambda i: (0, 0)),
            pl.BlockSpec(w.shape, lambda i: (0, 0, 0)),
            pl.BlockSpec((1, d), lambda i: (0, 0)),
        ],
        out_specs=pl.BlockSpec((tm, d), lambda i: (i, 0)),
        out_shape=jax.ShapeDtypeStruct((tokens, d), F32),
        scratch_shapes=[pltpu.VMEM((HALO + tm, d), F32)],
        compiler_params=_params(("arbitrary",)),
    )(x, x, norm, w, scale)


def _ffn_kernel(x_ref, g_ref, wg_ref, wu_ref, wd_ref, fg_ref, o_ref, hn_ref, *, final_norm):
    j = pl.program_id(1)

    halves = [slice(r * FFN_SUB_ROWS, (r + 1) * FFN_SUB_ROWS) for r in range(x_ref.shape[0] // FFN_SUB_ROWS)]

    def chunk(first):
        for rows in halves:
            if first:
                hn = _rms(x_ref[rows, :], g_ref[...]).astype(BF16)
                hn_ref[rows, :] = hn
            else:
                hn = hn_ref[rows, :]
            gate = jnp.dot(hn, wg_ref[...], preferred_element_type=F32)
            up = jnp.dot(hn, wu_ref[...], preferred_element_type=F32)
            act = gate / (1.0 + jnp.exp(-gate)) * up
            down = jnp.dot(act.astype(BF16), wd_ref[...], preferred_element_type=F32)
            o_ref[rows, :] = (x_ref[rows, :] if first else o_ref[rows, :]) + down

    pl.when(j == 0)(functools.partial(chunk, True))
    pl.when(j > 0)(functools.partial(chunk, False))

    if final_norm:
        @pl.when(j == pl.num_programs(1) - 1)
        def _():
            for rows in halves:
                o_ref[rows, :] = _rms(o_ref[rows, :], fg_ref[...])


def _ffn(x, norm, w_gate, w_up, w_down, final_g, *, layer, final_norm):
    tokens, d = x.shape
    n_chunks, tf = w_gate.shape[1], w_gate.shape[3]
    tm = FFN_ROW_TILE
    kern = functools.partial(_ffn_kernel, final_norm=final_norm)
    return pl.pallas_call(
        kern,
        name="ffn_final" if final_norm else "ffn",
        grid=(tokens // tm, n_chunks),
        in_specs=[
            pl.BlockSpec((tm, d), lambda i, j: (i, 0)),
            pl.BlockSpec((1, d), lambda i, j: (0, 0)),
            pl.BlockSpec((None, None, d, tf), lambda i, j: (layer, j, 0, 0)),
            pl.BlockSpec((None, None, d, tf), lambda i, j: (layer, j, 0, 0)),
            pl.BlockSpec((None, tf, d), lambda i, j: (layer, j, 0)),
            pl.BlockSpec((1, d), lambda i, j: (0, 0)),
        ],
        out_specs=pl.BlockSpec((tm, d), lambda i, j: (i, 0)),
        out_shape=jax.ShapeDtypeStruct((tokens, d), F32),
        scratch_shapes=[pltpu.VMEM((tm, d), BF16)],
        compiler_params=_params(("arbitrary", "arbitrary")),
    )(x, norm, w_gate, w_up, w_down, final_g)


def _kv_kernel(x_ref, g_ref, wa_ref, gl_ref, wkn_ref, wvt_ref, cos_ref, sin_ref, kn_ref, vt_ref, kpe_ref):
    h = _rms(x_ref[...], g_ref[...]).astype(BF16)
    kv = jnp.dot(h, wa_ref[...], preferred_element_type=F32)
    c_kv = _rms(kv[:, :KV_LORA_RANK], gl_ref[...]).astype(BF16)
    pe = kv[:, KV_LORA_RANK:KV_LORA_RANK + LANE]
    pe_swapped = kv[:, KV_LORA_RANK + LANE:]
    rot = pe * cos_ref[...] + pe_swapped * sin_ref[...]
    kpe_ref[:, :LANE] = rot.astype(BF16)
    kpe_ref[:, LANE:] = pltpu.roll(rot, LANE // 2, axis=1).astype(BF16)
    kn_ref[...] = jnp.dot(c_kv, wkn_ref[...], preferred_element_type=F32).astype(BF16)
    vt = lax.dot_general(wvt_ref[...], c_kv, (((1,), (1,)), ((), ())), preferred_element_type=F32)
    vt_ref[0] = vt.astype(BF16)


def _kv_proj(x, norm, wa_ext, latent_norm, w_kn, w_vt, cos, sin):
    tokens, d = x.shape
    tm = ATTN_TILE
    n_kn, n_v = w_kn.shape[1], w_vt.shape[0]
    return pl.pallas_call(
        _kv_kernel,
        name="kv_proj",
        grid=(tokens // tm,),
        in_specs=[
            pl.BlockSpec((tm, d), lambda i: (i, 0)),
            pl.BlockSpec((1, d), lambda i: (0, 0)),
            pl.BlockSpec(wa_ext.shape, lambda i: (0, 0)),
            pl.BlockSpec((1, KV_LORA_RANK), lambda i: (0, 0)),
            pl.BlockSpec(w_kn.shape, lambda i: (0, 0)),
            pl.BlockSpec(w_vt.shape, lambda i: (0, 0)),
            pl.BlockSpec((tm, LANE), lambda i: (i, 0)),
            pl.BlockSpec((tm, LANE), lambda i: (i, 0)),
        ],
        out_specs=[
            pl.BlockSpec((tm, n_kn), lambda i: (i, 0)),
            pl.BlockSpec((1, n_v, tm), lambda i: (i, 0, 0)),
            pl.BlockSpec((tm, 2 * LANE), lambda i: (i, 0)),
        ],
        out_shape=[
            jax.ShapeDtypeStruct((tokens, n_kn), BF16),
            jax.ShapeDtypeStruct((tokens // tm, n_v, tm), BF16),
            jax.ShapeDtypeStruct((tokens, 2 * LANE), BF16),
        ],
        compiler_params=_params(("arbitrary",)),
    )(x, norm, wa_ext, latent_norm, w_kn, w_vt, cos, sin)


def _q_kernel(x_ref, g_ref, wa_ref, gl_ref, wb_ref, cos_ref, sin_ref, qn_ref, qp_ref, *, n_nope, n_pe):
    h = _rms(x_ref[...], g_ref[...]).astype(BF16)
    cq = jnp.dot(h, wa_ref[...], preferred_element_type=F32)
    cqn = _rms(cq, gl_ref[...]).astype(BF16)
    qt = lax.dot_general(wb_ref[...], cqn, (((1,), (1,)), ((), ())), preferred_element_type=F32)
    qn_ref[...] = (qt[:n_nope, :] * Q_PRESCALE).astype(BF16)
    cos = cos_ref[...].T * Q_PRESCALE
    sin = sin_ref[...].T * Q_PRESCALE
    for p in range(n_pe // LANE):
        pe = qt[n_nope + p * LANE:n_nope + (p + 1) * LANE, :]
        sw = qt[n_nope + n_pe + p * LANE:n_nope + n_pe + (p + 1) * LANE, :]
        qp_ref[p * LANE:(p + 1) * LANE, :] = (pe * cos + sw * sin).astype(BF16)


def _q_proj(x, norm, wa, latent_norm, wb_ext, cos, sin):
    tokens, d = x.shape
    tm = ROW_TILE
    n_nope = N_HEADS * QK_NOPE_DIM
    n_pe = N_HEADS * QK_ROPE_DIM
    kern = functools.partial(_q_kernel, n_nope=n_nope, n_pe=n_pe)
    return pl.pallas_call(
        kern,
        name="q_proj",
        grid=(tokens // tm,),
        in_specs=[
            pl.BlockSpec((tm, d), lambda i: (i, 0)),
            pl.BlockSpec((1, d), lambda i: (0, 0)),
            pl.BlockSpec(wa.shape, lambda i: (0, 0)),
            pl.BlockSpec((1, wa.shape[1]), lambda i: (0, 0)),
            pl.BlockSpec(wb_ext.shape, lambda i: (0, 0)),
            pl.BlockSpec((tm, LANE), lambda i: (i, 0)),
            pl.BlockSpec((tm, LANE), lambda i: (i, 0)),
        ],
        out_specs=[
            pl.BlockSpec((n_nope, tm), lambda i: (0, i)),
            pl.BlockSpec((n_pe, tm), lambda i: (0, i)),
        ],
        out_shape=[
            jax.ShapeDtypeStruct((n_nope, tokens), BF16),
            jax.ShapeDtypeStruct((n_pe, tokens), BF16),
        ],
        compiler_params=_params(("arbitrary",)),
    )(x, norm, wa, latent_norm, wb_ext, cos, sin)


def _attn_kernel(qn_ref, qp_ref, kn_ref, kp_ref, vt_ref, o_ref, st_ref, acc_ref, *, blk, n_sub):
    qi = pl.program_id(2)
    base = n_sub * qi
    qp = qp_ref[...]
    qts = [jnp.concatenate([qn_ref[h * LANE:(h + 1) * LANE, :], qp], axis=0) for h in range(2)]
    acc_ref[...] = jnp.zeros(acc_ref.shape, F32)

    def scores(h, j, first_sub):
        off = pl.multiple_of(j * blk, blk)
        k = jnp.concatenate([kn_ref[pl.ds(off, blk), h * LANE:(h + 1) * LANE],
                             kp_ref[pl.ds(off, blk), h * LANE:(h + 1) * LANE]], axis=1)
        return jnp.dot(k, qts[h][:, first_sub * blk:], preferred_element_type=F32)

    ones_rows = jnp.ones((SUM_ROWS, blk), BF16)

    def consume(h, s, slot, vblk, m, cmax):
        m_new = jnp.maximum(m, cmax)
        alpha = jnp.exp2(m - m_new)
        pt = jnp.exp2(st_ref[h, s, slot] - m_new)
        vt = jnp.concatenate([vt_ref[vblk, h * V_DIM:(h + 1) * V_DIM, :], ones_rows], axis=0)
        acc_ref[h, s] = alpha * acc_ref[h, s] + jnp.dot(vt, pt.astype(BF16), preferred_element_type=F32)
        return m_new

    def colmax(st):
        return jnp.max(st, axis=0, keepdims=True)

    def step(t, carry, parity, first_sub=0, mask=None):
        carry = list(carry)
        for h in range(2):
            st = scores(h, t, first_sub)
            for s in range(max(first_sub - 1, 0), n_sub):
                m, cmax = carry[n_sub * h + s]
                m = consume(h, s, 1 - parity, t - 1, m, cmax)
                if s >= first_sub:
                    st_s = st[:, (s - first_sub) * blk:(s - first_sub + 1) * blk]
                    if s == first_sub and mask is not None:
                        st_s = jnp.where(mask, st_s, -jnp.inf)
                    st_ref[h, s, parity] = st_s
                    cmax = colmax(st_s)
                carry[n_sub * h + s] = (m, cmax)
        return tuple(carry)

    key = lax.broadcasted_iota(jnp.int32, (blk, blk), 0)
    qry = lax.broadcasted_iota(jnp.int32, (blk, blk), 1)
    causal = key <= qry
    m0 = jnp.full((1, blk), -jnp.inf, F32)

    first_mask = jnp.logical_or(causal, qi > 0)
    carry = []
    for h in range(2):
        st = scores(h, 0, 0)
        for s in range(n_sub):
            st_s = st[:, s * blk:(s + 1) * blk]
            if s == 0:
                st_s = jnp.where(first_mask, st_s, -jnp.inf)
            st_ref[h, s, 0] = st_s
            carry.append((m0, colmax(st_s)))

    def pair(u, carry):
        return step(2 * u + 2, step(2 * u + 1, carry, 1), 0)

    carry = lax.fori_loop(0, jnp.maximum(base // 2 - 1, 0), pair, tuple(carry))
    carry = lax.cond(qi >= 1,
                     lambda c: step(base, step(base - 1, c, 1), 0, 0, causal),
                     lambda c: c, carry)

    for k in range(1, n_sub):
        carry = step(base + k, carry, k % 2, k, causal)

    last = n_sub - 1
    for h in range(2):
        consume(h, last, last % 2, base + last, *carry[n_sub * h + last])
        for s in range(n_sub):
            out = acc_ref[h, s, :V_DIM, :] / acc_ref[h, s, V_DIM:V_DIM + 1, :]
            o_ref[s * blk:(s + 1) * blk, h * LANE:(h + 1) * LANE] = out.T.astype(BF16)


def _attention(qn, qp, kn, kpe, vt, batch, seq):
    tokens = qn.shape[1]
    blk, n_sub = ATTN_TILE, ATTN_SUB_TILES
    rows = n_sub * blk
    nk = seq // blk
    nq = seq // rows
    kern = functools.partial(_attn_kernel, blk=blk, n_sub=n_sub)
    return pl.pallas_call(
        kern,
        name="attention",
        grid=(batch, N_HEADS // 2, nq),
        in_specs=[
            pl.BlockSpec((2 * LANE, rows), lambda b, p, i: (p, b * nq + i)),
            pl.BlockSpec((LANE, rows), lambda b, p, i: (p, b * nq + i)),
            pl.BlockSpec((seq, 2 * LANE), lambda b, p, i: (b, p)),
            pl.BlockSpec((seq, 2 * LANE), lambda b, p, i: (b, 0)),
            pl.BlockSpec((nk, 2 * V_DIM, blk), lambda b, p, i: (b, p, 0)),
        ],
        out_specs=pl.BlockSpec((rows, 2 * LANE), lambda b, p, i: (b * nq + i, p)),
        out_shape=jax.ShapeDtypeStruct((tokens, N_HEADS * V_DIM), BF16),
        scratch_shapes=[pltpu.VMEM((2, n_sub, 2, blk, blk), F32),
                        pltpu.VMEM((2, n_sub, V_DIM + SUM_ROWS, blk), F32)],
        compiler_params=_params(("arbitrary", "arbitrary", "arbitrary")),
    )(qn, qp, kn, kpe, vt)


def _out_proj_kernel(x_ref, a_ref, w_ref, o_ref):
    o_ref[...] = x_ref[...] + jnp.dot(a_ref[...], w_ref[...], preferred_element_type=F32)


def _out_proj(x, attn, w_o):
    tokens, d = x.shape
    tm = ROW_TILE
    return pl.pallas_call(
        _out_proj_kernel,
        name="out_proj",
        grid=(tokens // tm,),
        in_specs=[
            pl.BlockSpec((tm, d), lambda i: (i, 0)),
            pl.BlockSpec((tm, attn.shape[1]), lambda i: (i, 0)),
            pl.BlockSpec(w_o.shape, lambda i: (0, 0)),
        ],
        out_specs=pl.BlockSpec((tm, d), lambda i: (i, 0)),
        out_shape=jax.ShapeDtypeStruct((tokens, d), F32),
        compiler_params=_params(("arbitrary",)),
    )(x, attn, w_o)


def _swap_halves(w):
    half = w.shape[-1] // 2
    return jnp.concatenate([w[..., half:], w[..., :half]], axis=-1)


def _rope_tables(positions):
    half = QK_ROPE_DIM // 2
    inv_freq = jnp.tile(ROPE_BASE ** (-jnp.arange(half, dtype=F32) / half), LANE // half)
    sign = jnp.tile(jnp.concatenate([-jnp.ones((half,), F32), jnp.ones((half,), F32)]), LANE // QK_ROPE_DIM)
    pos = positions.astype(F32).reshape(-1)
    ang = pos[:, None] * inv_freq[None, :]
    return jnp.cos(ang), jnp.sin(ang) * sign[None, :]


def kernel(x, positions, pool_norm, pool_w, pool_scale, kv_in_norm, w_kv_a, kv_latent_norm, w_kv_b, attn_norm, w_q_a, q_latent_norm, w_q_b, w_o, ffn_norm, w_gate, w_up, w_down, final_norm):
    batch, seq, d = x.shape
    depth = ffn_norm.shape[0]
    n_pool = pool_norm.shape[0]
    assert seq % ROW_TILE == 0 and seq % (ATTN_SUB_TILES * ATTN_TILE) == 0 and ATTN_SUB_TILES % 2 == 0
    assert w_gate.shape[2] % FF_TILE == 0 and seq % FFN_ROW_TILE == 0

    cos, sin = _rope_tables(positions)
    row = lambda v: v.reshape(1, -1)
    xs = x.reshape(batch * seq, d)

    def chunked(w):
        return w.astype(BF16).reshape(depth, d, -1, FF_TILE).transpose(0, 2, 1, 3)

    wg, wu, wd = chunked(w_gate), chunked(w_up), w_down.astype(BF16)
    zpad = jnp.zeros((d, LANE - QK_ROPE_DIM), F32)
    w_pe = w_kv_a[:, KV_LORA_RANK:]
    wa_ext = jnp.concatenate([w_kv_a[:, :KV_LORA_RANK], w_pe, zpad, _swap_halves(w_pe), zpad], axis=1).astype(BF16)
    wkb = w_kv_b.reshape(KV_LORA_RANK, N_HEADS, QK_NOPE_DIM + V_DIM)
    w_kn = wkb[:, :, :QK_NOPE_DIM].reshape(KV_LORA_RANK, -1).astype(BF16)
    w_vt = wkb[:, :, QK_NOPE_DIM:].reshape(KV_LORA_RANK, -1).T.astype(BF16)

    kn = vt = kpe = None
    for l in range(depth):
        if l == n_pool:
            kn, vt, kpe = _kv_proj(xs, row(kv_in_norm), wa_ext, row(kv_latent_norm), w_kn, w_vt, cos, sin)
        if l < n_pool:
            xs = _pool_mixer(xs, seq, row(pool_norm[l]), pool_w[l].astype(BF16), row(pool_scale[l]))
        else:
            b = l - n_pool
            rank = w_q_b.shape[1]
            wqb = w_q_b[b].reshape(rank, N_HEADS, QK_DIM)
            wqb_pe = wqb[:, :, QK_NOPE_DIM:]
            wqb_ext = jnp.concatenate([
                wqb[:, :, :QK_NOPE_DIM].reshape(rank, -1),
                wqb_pe.reshape(rank, -1),
                _swap_halves(wqb_pe).reshape(rank, -1)], axis=1).T.astype(BF16)
            qn, qp = _q_proj(xs, row(attn_norm[b]), w_q_a[b].astype(BF16), row(q_latent_norm[b]), wqb_ext, cos, sin)
            attn = _attention(qn, qp, kn, kpe, vt, batch, seq)
            xs = _out_proj(xs, attn, w_o[b].astype(BF16))
        last = l == depth - 1
        xs = _ffn(xs, row(ffn_norm[l]), wg, wu, wd, row(final_norm), layer=l, final_norm=last)
    return xs.reshape(batch, seq, d)
```

```python
import functools
import math

import jax
import jax.numpy as jnp
from jax import lax
from jax.experimental import pallas as pl
from jax.experimental.pallas import tpu as pltpu

N_HEADS = 16
QK_NOPE_DIM = 128
QK_ROPE_DIM = 64
QK_DIM = QK_NOPE_DIM + QK_ROPE_DIM
V_DIM = 128
KV_LORA_RANK = 512
POOL_WINDOWS = (2, 4, 8, 16)
ROPE_BASE = 10000.0
NORM_EPS = 1e-6

LANE = 128
SUBLANE = 8
assert POOL_WINDOWS == tuple(2 ** (k + 1) for k in range(len(POOL_WINDOWS)))
HALO = SUBLANE * len(POOL_WINDOWS)
SUM_ROWS = 16
Q_PRESCALE = (1.0 / math.sqrt(QK_DIM)) * math.log2(math.e)

ROW_TILE = 512
FF_TILE = 512
FFN_ROW_TILE = 1024
FFN_SUB_ROWS = 512
ATTN_TILE = 512
ATTN_SUB_TILES = 2
VMEM_LIMIT = 56 * 1024 * 1024

F32 = jnp.float32
BF16 = jnp.bfloat16


def _rms(x, g):
    ms = jnp.mean(x * x, axis=-1, keepdims=True)
    return x * lax.rsqrt(ms + NORM_EPS) * g


def _params(semantics, flags=None):
    return pltpu.CompilerParams(dimension_semantics=semantics, vmem_limit_bytes=VMEM_LIMIT, flags=flags)


def _pool_kernel(x_ref, halo_ref, g_ref, w_ref, sc_ref, o_ref, hh_ref, lv_ref, *, tiles_per_seq, tm, pg):
    t = pl.program_id(0) % tiles_per_seq
    g = g_ref[...]
    rows_end = HALO + tm
    hh_ref[HALO:, :] = _rms(x_ref[...], g)
    hh_ref[:HALO, :] = jnp.where(t == 0, 0.0, _rms(halo_ref[...], g))
    for k in range(1, len(POOL_WINDOWS) + 1):
        start, shift, cols = SUBLANE * k, 2 ** (k - 1), slice((k - 1) * pg, None)
        src = hh_ref if k == 1 else lv_ref.at[k % 2]
        lv_ref[(k - 1) % 2, start:rows_end, cols] = (src[start:rows_end, cols]
                                                     + src[start - shift:rows_end - shift, cols])
    pos = t * tm + lax.broadcasted_iota(jnp.int32, (tm, 1), 0)
    for gi, w in enumerate(POOL_WINDOWS):
        cols = slice(gi * pg, (gi + 1) * pg)
        count = jnp.minimum(pos + 1, w).astype(F32)
        diff = lv_ref[gi % 2, HALO:, cols] / count - hh_ref[HALO:, cols]
        y = jnp.dot(diff.astype(BF16), w_ref[gi], preferred_element_type=F32)
        o_ref[:, cols] = x_ref[:, cols] + y * sc_ref[:, cols]


def _pool_mixer(x, seq, norm, w, scale):
    tokens, d = x.shape
    tm = ROW_TILE
    pg = d // len(POOL_WINDOWS)
    kern = functools.partial(_pool_kernel, tiles_per_seq=seq // tm, tm=tm, pg=pg)
    return pl.pallas_call(
        kern,
        name="pool_mixer",
        grid=(tokens // tm,),
        in_specs=[
            pl.BlockSpec((tm, d), lambda i: (i, 0)),
            pl.BlockSpec((HALO, d), lambda i: (jnp.maximum(i * (tm // HALO) - 1, 0), 0)),
            pl.BlockSpec((1, d), lambda i: (0, 0)),
            pl.BlockSpec(w.shape, lambda i: (0, 0, 0)),
            pl.BlockSpec((1, d), lambda i: (0, 0)),
        ],
        out_specs=pl.BlockSpec((tm, d), lambda i: (i, 0)),
        out_shape=jax.ShapeDtypeStruct((tokens, d), F32),
        scratch_shapes=[pltpu.VMEM((HALO + tm, d), F32), pltpu.VMEM((2, HALO + tm, d), F32)],
        compiler_params=_params(("arbitrary",)),
    )(x, x, norm, w, scale)


def _ffn_kernel(x_ref, g_ref, wg_ref, wu_ref, wd_ref, fg_ref, o_ref, hn_ref, *, final_norm):
    j = pl.program_id(1)

    halves = [slice(r * FFN_SUB_ROWS, (r + 1) * FFN_SUB_ROWS) for r in range(x_ref.shape[0] // FFN_SUB_ROWS)]

    def chunk(first):
        for rows in halves:
            if first:
                hn = _rms(x_ref[rows, :], g_ref[...]).astype(BF16)
                hn_ref[rows, :] = hn
            else:
                hn = hn_ref[rows, :]
            gate = jnp.dot(hn, wg_ref[...], preferred_element_type=F32)
            up = jnp.dot(hn, wu_ref[...], preferred_element_type=F32)
            act = gate / (1.0 + jnp.exp(-gate)) * up
            down = jnp.dot(act.astype(BF16), wd_ref[...], preferred_element_type=F32)
            o_ref[rows, :] = (x_ref[rows, :] if first else o_ref[rows, :]) + down

    pl.when(j == 0)(functools.partial(chunk, True))
    pl.when(j > 0)(functools.partial(chunk, False))

    if final_norm:
        @pl.when(j == pl.num_programs(1) - 1)
        def _():
            for rows in halves:
                o_ref[rows, :] = _rms(o_ref[rows, :], fg_ref[...])


def _ffn(x, norm, w_gate, w_up, w_down, final_g, *, layer, final_norm):
    tokens, d = x.shape
    dff = w_gate.shape[2]
    tm, tf = FFN_ROW_TILE, FF_TILE
    kern = functools.partial(_ffn_kernel, final_norm=final_norm)
    return pl.pallas_call(
        kern,
        name="ffn_final" if final_norm else "ffn",
        grid=(tokens // tm, dff // tf),
        in_specs=[
            pl.BlockSpec((tm, d), lambda i, j: (i, 0)),
            pl.BlockSpec((1, d), lambda i, j: (0, 0)),
            pl.BlockSpec((None, d, tf), lambda i, j: (layer, 0, j)),
            pl.BlockSpec((None, d, tf), lambda i, j: (layer, 0, j)),
            pl.BlockSpec((None, tf, d), lambda i, j: (layer, j, 0)),
            pl.BlockSpec((1, d), lambda i, j: (0, 0)),
        ],
        out_specs=pl.BlockSpec((tm, d), lambda i, j: (i, 0)),
        out_shape=jax.ShapeDtypeStruct((tokens, d), F32),
        scratch_shapes=[pltpu.VMEM((tm, d), BF16)],
        compiler_params=_params(("arbitrary", "arbitrary")),
    )(x, norm, w_gate, w_up, w_down, final_g)


def _kv_kernel(x_ref, g_ref, wa_ref, gl_ref, wkn_ref, wvt_ref, cos_ref, sin_ref, kn_ref, vt_ref, kpe_ref):
    h = _rms(x_ref[...], g_ref[...]).astype(BF16)
    kv = jnp.dot(h, wa_ref[...], preferred_element_type=F32)
    c_kv = _rms(kv[:, :KV_LORA_RANK], gl_ref[...]).astype(BF16)
    pe = kv[:, KV_LORA_RANK:KV_LORA_RANK + LANE]
    pe_swapped = kv[:, KV_LORA_RANK + LANE:]
    rot = pe * cos_ref[...] + pe_swapped * sin_ref[...]
    kpe_ref[:, :LANE] = rot.astype(BF16)
    kpe_ref[:, LANE:] = pltpu.roll(rot, LANE // 2, axis=1).astype(BF16)
    kn_ref[...] = jnp.dot(c_kv, wkn_ref[...], preferred_element_type=F32).astype(BF16)
    vt = lax.dot_general(wvt_ref[...], c_kv, (((1,), (1,)), ((), ())), preferred_element_type=F32)
    vt_ref[0] = vt.astype(BF16)


def _kv_proj(x, norm, wa_ext, latent_norm, w_kn, w_vt, cos, sin):
    tokens, d = x.shape
    tm = ATTN_TILE
    n_kn, n_v = w_kn.shape[1], w_vt.shape[0]
    return pl.pallas_call(
        _kv_kernel,
        name="kv_proj",
        grid=(tokens // tm,),
        in_specs=[
            pl.BlockSpec((tm, d), lambda i: (i, 0)),
            pl.BlockSpec((1, d), lambda i: (0, 0)),
            pl.BlockSpec(wa_ext.shape, lambda i: (0, 0)),
            pl.BlockSpec((1, KV_LORA_RANK), lambda i: (0, 0)),
            pl.BlockSpec(w_kn.shape, lambda i: (0, 0)),
            pl.BlockSpec(w_vt.shape, lambda i: (0, 0)),
            pl.BlockSpec((tm, LANE), lambda i: (i, 0)),
            pl.BlockSpec((tm, LANE), lambda i: (i, 0)),
        ],
        out_specs=[
            pl.BlockSpec((tm, n_kn), lambda i: (i, 0)),
            pl.BlockSpec((1, n_v, tm), lambda i: (i, 0, 0)),
            pl.BlockSpec((tm, 2 * LANE), lambda i: (i, 0)),
        ],
        out_shape=[
            jax.ShapeDtypeStruct((tokens, n_kn), BF16),
            jax.ShapeDtypeStruct((tokens // tm, n_v, tm), BF16),
            jax.ShapeDtypeStruct((tokens, 2 * LANE), BF16),
        ],
        compiler_params=_params(("arbitrary",)),
    )(x, norm, wa_ext, latent_norm, w_kn, w_vt, cos, sin)


def _q_kernel(x_ref, g_ref, wa_ref, gl_ref, wb_ref, cos_ref, sin_ref, qn_ref, qp_ref, *, n_nope, n_pe):
    h = _rms(x_ref[...], g_ref[...]).astype(BF16)
    cq = jnp.dot(h, wa_ref[...], preferred_element_type=F32)
    cqn = _rms(cq, gl_ref[...]).astype(BF16)
    qt = lax.dot_general(wb_ref[...], cqn, (((1,), (1,)), ((), ())), preferred_element_type=F32)
    qn_ref[...] = (qt[:n_nope, :] * Q_PRESCALE).astype(BF16)
    cos = cos_ref[...].T * Q_PRESCALE
    sin = sin_ref[...].T * Q_PRESCALE
    for p in range(n_pe // LANE):
        pe = qt[n_nope + p * LANE:n_nope + (p + 1) * LANE, :]
        sw = qt[n_nope + n_pe + p * LANE:n_nope + n_pe + (p + 1) * LANE, :]
        qp_ref[p * LANE:(p + 1) * LANE, :] = (pe * cos + sw * sin).astype(BF16)


def _q_proj(x, norm, wa, latent_norm, wb_ext, cos, sin):
    tokens, d = x.shape
    tm = ROW_TILE
    n_nope = N_HEADS * QK_NOPE_DIM
    n_pe = N_HEADS * QK_ROPE_DIM
    kern = functools.partial(_q_kernel, n_nope=n_nope, n_pe=n_pe)
    return pl.pallas_call(
        kern,
        name="q_proj",
        grid=(tokens // tm,),
        in_specs=[
            pl.BlockSpec((tm, d), lambda i: (i, 0)),
            pl.BlockSpec((1, d), lambda i: (0, 0)),
            pl.BlockSpec(wa.shape, lambda i: (0, 0)),
            pl.BlockSpec((1, wa.shape[1]), lambda i: (0, 0)),
            pl.BlockSpec(wb_ext.shape, lambda i: (0, 0)),
            pl.BlockSpec((tm, LANE), lambda i: (i, 0)),
            pl.BlockSpec((tm, LANE), lambda i: (i, 0)),
        ],
        out_specs=[
            pl.BlockSpec((n_nope, tm), lambda i: (0, i)),
            pl.BlockSpec((n_pe, tm), lambda i: (0, i)),
        ],
        out_shape=[
            jax.ShapeDtypeStruct((n_nope, tokens), BF16),
            jax.ShapeDtypeStruct((n_pe, tokens), BF16),
        ],
        compiler_params=_params(("arbitrary",)),
    )(x, norm, wa, latent_norm, wb_ext, cos, sin)


def _attn_kernel(qn_ref, qp_ref, kn_ref, kp_ref, vt_ref, o_ref, st_ref, acc_ref, *, blk, n_sub):
    qi = pl.program_id(2)
    base = n_sub * qi
    qp = qp_ref[...]
    qts = [jnp.concatenate([qn_ref[h * LANE:(h + 1) * LANE, :], qp], axis=0) for h in range(2)]
    acc_ref[...] = jnp.zeros(acc_ref.shape, F32)

    def scores(h, j, first_sub):
        off = pl.multiple_of(j * blk, blk)
        k = jnp.concatenate([kn_ref[pl.ds(off, blk), h * LANE:(h + 1) * LANE],
                             kp_ref[pl.ds(off, blk), h * LANE:(h + 1) * LANE]], axis=1)
        return jnp.dot(k, qts[h][:, first_sub * blk:], preferred_element_type=F32)

    ones_rows = jnp.ones((SUM_ROWS, blk), BF16)

    def consume(h, s, slot, vblk, m, cmax):
        m_new = jnp.maximum(m, cmax)
        alpha = jnp.exp2(m - m_new)
        pt = jnp.exp2(st_ref[h, s, slot] - m_new)
        vt = jnp.concatenate([vt_ref[vblk, h * V_DIM:(h + 1) * V_DIM, :], ones_rows], axis=0)
        acc_ref[h, s] = alpha * acc_ref[h, s] + jnp.dot(vt, pt.astype(BF16), preferred_element_type=F32)
        return m_new

    def colmax(st):
        return jnp.max(st, axis=0, keepdims=True)

    def step(t, carry, parity, first_sub=0, mask=None):
        carry = list(carry)
        for h in range(2):
            st = scores(h, t, first_sub)
            for s in range(max(first_sub - 1, 0), n_sub):
                m, cmax = carry[n_sub * h + s]
                m = consume(h, s, 1 - parity, t - 1, m, cmax)
                if s >= first_sub:
                    st_s = st[:, (s - first_sub) * blk:(s - first_sub + 1) * blk]
                    if s == first_sub and mask is not None:
                        st_s = jnp.where(mask, st_s, -jnp.inf)
                    st_ref[h, s, parity] = st_s
                    cmax = colmax(st_s)
                carry[n_sub * h + s] = (m, cmax)
        return tuple(carry)

    key = lax.broadcasted_iota(jnp.int32, (blk, blk), 0)
    qry = lax.broadcasted_iota(jnp.int32, (blk, blk), 1)
    causal = key <= qry
    m0 = jnp.full((1, blk), -jnp.inf, F32)

    first_mask = jnp.logical_or(causal, qi > 0)
    carry = []
    for h in range(2):
        st = scores(h, 0, 0)
        for s in range(n_sub):
            st_s = st[:, s * blk:(s + 1) * blk]
            if s == 0:
                st_s = jnp.where(first_mask, st_s, -jnp.inf)
            st_ref[h, s, 0] = st_s
            carry.append((m0, colmax(st_s)))

    def pair(u, carry):
        return step(2 * u + 2, step(2 * u + 1, carry, 1), 0)

    carry = lax.fori_loop(0, jnp.maximum(base // 2 - 1, 0), pair, tuple(carry))
    carry = lax.cond(qi >= 1,
                     lambda c: step(base, step(base - 1, c, 1), 0, 0, causal),
                     lambda c: c, carry)

    for k in range(1, n_sub):
        carry = step(base + k, carry, k % 2, k, causal)

    last = n_sub - 1
    for h in range(2):
        consume(h, last, last % 2, base + last, *carry[n_sub * h + last])
        for s in range(n_sub):
            out = acc_ref[h, s, :V_DIM, :] / acc_ref[h, s, V_DIM:V_DIM + 1, :]
            o_ref[s * blk:(s + 1) * blk, h * LANE:(h + 1) * LANE] = out.T.astype(BF16)


def _attention(qn, qp, kn, kpe, vt, batch, seq):
    tokens = qn.shape[1]
    blk, n_sub = ATTN_TILE, ATTN_SUB_TILES
    rows = n_sub * blk
    nk = seq // blk
    nq = seq // rows
    kern = functools.partial(_attn_kernel, blk=blk, n_sub=n_sub)
    return pl.pallas_call(
        kern,
        name="attention",
        grid=(batch, N_HEADS // 2, nq),
        in_specs=[
            pl.BlockSpec((2 * LANE, rows), lambda b, p, i: (p, b * nq + i)),
            pl.BlockSpec((LANE, rows), lambda b, p, i: (p, b * nq + i)),
            pl.BlockSpec((seq, 2 * LANE), lambda b, p, i: (b, p)),
            pl.BlockSpec((seq, 2 * LANE), lambda b, p, i: (b, 0)),
            pl.BlockSpec((nk, 2 * V_DIM, blk), lambda b, p, i: (b, p, 0)),
        ],
        out_specs=pl.BlockSpec((rows, 2 * LANE), lambda b, p, i: (b * nq + i, p)),
        out_shape=jax.ShapeDtypeStruct((tokens, N_HEADS * V_DIM), BF16),
        scratch_shapes=[pltpu.VMEM((2, n_sub, 2, blk, blk), F32),
                        pltpu.VMEM((2, n_sub, V_DIM + SUM_ROWS, blk), F32)],
        compiler_params=_params(("arbitrary", "arbitrary", "arbitrary")),
    )(qn, qp, kn, kpe, vt)


def _out_proj_kernel(x_ref, a_ref, w_ref, o_ref):
    o_ref[...] = x_ref[...] + jnp.dot(a_ref[...], w_ref[...], preferred_element_type=F32)


def _out_proj(x, attn, w_o):
    tokens, d = x.shape
    tm = ROW_TILE
    return pl.pallas_call(
        _out_proj_kernel,
        name="out_proj",
        grid=(tokens // tm,),
        in_specs=[
            pl.BlockSpec((tm, d), lambda i: (i, 0)),
            pl.BlockSpec((tm, attn.shape[1]), lambda i: (i, 0)),
            pl.BlockSpec(w_o.shape, lambda i: (0, 0)),
        ],
        out_specs=pl.BlockSpec((tm, d), lambda i: (i, 0)),
        out_shape=jax.ShapeDtypeStruct((tokens, d), F32),
        compiler_params=_params(("arbitrary",)),
    )(x, attn, w_o)


def _swap_halves(w):
    half = w.shape[-1] // 2
    return jnp.concatenate([w[..., half:], w[..., :half]], axis=-1)


def _rope_tables(positions):
    half = QK_ROPE_DIM // 2
    inv_freq = jnp.tile(ROPE_BASE ** (-jnp.arange(half, dtype=F32) / half), LANE // half)
    sign = jnp.tile(jnp.concatenate([-jnp.ones((half,), F32), jnp.ones((half,), F32)]), LANE // QK_ROPE_DIM)
    pos = positions.astype(F32).reshape(-1)
    ang = pos[:, None] * inv_freq[None, :]
    return jnp.cos(ang), jnp.sin(ang) * sign[None, :]


def kernel(x, positions, pool_norm, pool_w, pool_scale, kv_in_norm, w_kv_a, kv_latent_norm, w_kv_b, attn_norm, w_q_a, q_latent_norm, w_q_b, w_o, ffn_norm, w_gate, w_up, w_down, final_norm):
    batch, seq, d = x.shape
    depth = ffn_norm.shape[0]
    n_pool = pool_norm.shape[0]
    assert seq % ROW_TILE == 0 and seq % (ATTN_SUB_TILES * ATTN_TILE) == 0 and ATTN_SUB_TILES % 2 == 0
    assert w_gate.shape[2] % FF_TILE == 0 and seq % FFN_ROW_TILE == 0

    cos, sin = _rope_tables(positions)
    row = lambda v: v.reshape(1, -1)
    xs = x.reshape(batch * seq, d)

    wg, wu, wd = w_gate.astype(BF16), w_up.astype(BF16), w_down.astype(BF16)
    zpad = jnp.zeros((d, LANE - QK_ROPE_DIM), F32)
    w_pe = w_kv_a[:, KV_LORA_RANK:]
    wa_ext = jnp.concatenate([w_kv_a[:, :KV_LORA_RANK], w_pe, zpad, _swap_halves(w_pe), zpad], axis=1).astype(BF16)
    wkb = w_kv_b.reshape(KV_LORA_RANK, N_HEADS, QK_NOPE_DIM + V_DIM)
    w_kn = wkb[:, :, :QK_NOPE_DIM].reshape(KV_LORA_RANK, -1).astype(BF16)
    w_vt = wkb[:, :, QK_NOPE_DIM:].reshape(KV_LORA_RANK, -1).T.astype(BF16)

    kn = vt = kpe = None
    for l in range(depth):
        if l == n_pool:
            kn, vt, kpe = _kv_proj(xs, row(kv_in_norm), wa_ext, row(kv_latent_norm), w_kn, w_vt, cos, sin)
        if l < n_pool:
            xs = _pool_mixer(xs, seq, row(pool_norm[l]), pool_w[l].astype(BF16), row(pool_scale[l]))
        else:
            b = l - n_pool
            rank = w_q_b.shape[1]
            wqb = w_q_b[b].reshape(rank, N_HEADS, QK_DIM)
            wqb_pe = wqb[:, :, QK_NOPE_DIM:]
            wqb_ext = jnp.concatenate([
                wqb[:, :, :QK_NOPE_DIM].reshape(rank, -1),
                wqb_pe.reshape(rank, -1),
                _swap_halves(wqb_pe).reshape(rank, -1)], axis=1).T.astype(BF16)
            qn, qp = _q_proj(xs, row(attn_norm[b]), w_q_a[b].astype(BF16), row(q_latent_norm[b]), wqb_ext, cos, sin)
            attn = _attention(qn, qp, kn, kpe, vt, batch, seq)
            xs = _out_proj(xs, attn, w_o[b].astype(BF16))
        last = l == depth - 1
        xs = _ffn(xs, row(ffn_norm[l]), wg, wu, wd, row(final_norm), layer=l, final_norm=last)
    return xs.reshape(batch, seq, d)
```

```python
import functools
import math

import jax
import jax.numpy as jnp
from jax import lax
from jax.experimental import pallas as pl
from jax.experimental.pallas import tpu as pltpu

N_HEADS = 16
QK_NOPE_DIM = 128
QK_ROPE_DIM = 64
QK_DIM = QK_NOPE_DIM + QK_ROPE_DIM
V_DIM = 128
KV_LORA_RANK = 512
POOL_WINDOWS = (2, 4, 8, 16)
ROPE_BASE = 10000.0
NORM_EPS = 1e-6

LANE = 128
SUBLANE = 8
assert POOL_WINDOWS == tuple(2 ** (k + 1) for k in range(len(POOL_WINDOWS)))
HALO = SUBLANE * len(POOL_WINDOWS)
SUM_ROWS = 16
Q_PRESCALE = (1.0 / math.sqrt(QK_DIM)) * math.log2(math.e)

ROW_TILE = 512
FF_TILE = 512
FFN_ROW_TILE = 1024
FFN_SUB_ROWS = 512
ATTN_TILE = 512
ATTN_SUB_TILES = 2
VMEM_LIMIT = 56 * 1024 * 1024

F32 = jnp.float32
BF16 = jnp.bfloat16


def _rms(x, g):
    ms = jnp.mean(x * x, axis=-1, keepdims=True)
    return x * lax.rsqrt(ms + NORM_EPS) * g


def _rope_cos_sin(pos_ref, invf_ref):
    ang = invf_ref[...] * pos_ref[...]
    return jnp.cos(ang), jnp.sin(ang)


def _params(semantics, flags=None):
    return pltpu.CompilerParams(dimension_semantics=semantics, vmem_limit_bytes=VMEM_LIMIT, flags=flags)


def _pool_kernel(x_ref, halo_ref, g_ref, w_ref, sc_ref, o_ref, hh_ref, lv_ref, *, tiles_per_seq, tm, pg):
    t = pl.program_id(0) % tiles_per_seq
    g = g_ref[...]
    rows_end = HALO + tm
    hh_ref[HALO:, :] = _rms(x_ref[...], g)
    hh_ref[:HALO, :] = jnp.where(t == 0, 0.0, _rms(halo_ref[...], g))
    for k in range(1, len(POOL_WINDOWS) + 1):
        start, shift, cols = SUBLANE * k, 2 ** (k - 1), slice((k - 1) * pg, None)
        src = hh_ref if k == 1 else lv_ref.at[k % 2]
        lv_ref[(k - 1) % 2, start:rows_end, cols] = (src[start:rows_end, cols]
                                                     + src[start - shift:rows_end - shift, cols])
    pos = t * tm + lax.broadcasted_iota(jnp.int32, (tm, 1), 0)
    for gi, w in enumerate(POOL_WINDOWS):
        cols = slice(gi * pg, (gi + 1) * pg)
        count = jnp.minimum(pos + 1, w).astype(F32)
        diff = lv_ref[gi % 2, HALO:, cols] / count - hh_ref[HALO:, cols]
        y = jnp.dot(diff.astype(BF16), w_ref[gi], preferred_element_type=F32)
        o_ref[:, cols] = x_ref[:, cols] + y * sc_ref[:, cols]


def _pool_mixer(x, seq, norm, w, scale):
    tokens, d = x.shape
    tm = ROW_TILE
    pg = d // len(POOL_WINDOWS)
    kern = functools.partial(_pool_kernel, tiles_per_seq=seq // tm, tm=tm, pg=pg)
    return pl.pallas_call(
        kern,
        name="pool_mixer",
        grid=(tokens // tm,),
        in_specs=[
            pl.BlockSpec((tm, d), lambda i: (i, 0)),
            pl.BlockSpec((HALO, d), lambda i: (jnp.maximum(i * (tm // HALO) - 1, 0), 0)),
            pl.BlockSpec((1, d), lambda i: (0, 0)),
            pl.BlockSpec(w.shape, lambda i: (0, 0, 0)),
            pl.BlockSpec((1, d), lambda i: (0, 0)),
        ],
        out_specs=pl.BlockSpec((tm, d), lambda i: (i, 0)),
        out_shape=jax.ShapeDtypeStruct((tokens, d), F32),
        scratch_shapes=[pltpu.VMEM((HALO + tm, d), F32), pltpu.VMEM((2, HALO + tm, d), F32)],
        compiler_params=_params(("arbitrary",)),
    )(x, x, norm, w, scale)


def _ffn_kernel(x_ref, g_ref, wg_ref, wu_ref, wd_ref, fg_ref, o_ref, hn_ref, *, final_norm):
    j = pl.program_id(1)

    halves = [slice(r * FFN_SUB_ROWS, (r + 1) * FFN_SUB_ROWS) for r in range(x_ref.shape[0] // FFN_SUB_ROWS)]

    def chunk(first):
        for rows in halves:
            if first:
                hn = _rms(x_ref[rows, :], g_ref[...]).astype(BF16)
                hn_ref[rows, :] = hn
            else:
                hn = hn_ref[rows, :]
            gate = jnp.dot(hn, wg_ref[...], preferred_element_type=F32)
            up = jnp.dot(hn, wu_ref[...], preferred_element_type=F32)
            act = gate / (1.0 + jnp.exp(-gate)) * up
            down = jnp.dot(act.astype(BF16), wd_ref[...], preferred_element_type=F32)
            o_ref[rows, :] = (x_ref[rows, :] if first else o_ref[rows, :]) + down

    pl.when(j == 0)(functools.partial(chunk, True))
    pl.when(j > 0)(functools.partial(chunk, False))

    if final_norm:
        @pl.when(j == pl.num_programs(1) - 1)
        def _():
            for rows in halves:
                o_ref[rows, :] = _rms(o_ref[rows, :], fg_ref[...])


def _ffn(x, norm, w_gate, w_up, w_down, final_g, *, layer, final_norm):
    tokens, d = x.shape
    dff = w_gate.shape[2]
    tm, tf = FFN_ROW_TILE, FF_TILE
    kern = functools.partial(_ffn_kernel, final_norm=final_norm)
    return pl.pallas_call(
        kern,
        name="ffn_final" if final_norm else "ffn",
        grid=(tokens // tm, dff // tf),
        in_specs=[
            pl.BlockSpec((tm, d), lambda i, j: (i, 0)),
            pl.BlockSpec((1, d), lambda i, j: (0, 0)),
            pl.BlockSpec((None, d, tf), lambda i, j: (layer, 0, j)),
            pl.BlockSpec((None, d, tf), lambda i, j: (layer, 0, j)),
            pl.BlockSpec((None, tf, d), lambda i, j: (layer, j, 0)),
            pl.BlockSpec((1, d), lambda i, j: (0, 0)),
        ],
        out_specs=pl.BlockSpec((tm, d), lambda i, j: (i, 0)),
        out_shape=jax.ShapeDtypeStruct((tokens, d), F32),
        scratch_shapes=[pltpu.VMEM((tm, d), BF16)],
        compiler_params=_params(("arbitrary", "arbitrary")),
    )(x, norm, w_gate, w_up, w_down, final_g)


def _kv_kernel(x_ref, g_ref, wa_ref, gl_ref, wkn_ref, wvt_ref, pos_ref, invf_ref, kn_ref, vt_ref, kpe_ref):
    h = _rms(x_ref[...], g_ref[...]).astype(BF16)
    kv = jnp.dot(h, wa_ref[...], preferred_element_type=F32)
    c_kv = _rms(kv[:, :KV_LORA_RANK], gl_ref[...]).astype(BF16)
    pe = kv[:, KV_LORA_RANK:KV_LORA_RANK + LANE]
    pe_swapped = kv[:, KV_LORA_RANK + LANE:]
    cos_t, sin_t = _rope_cos_sin(pos_ref, invf_ref)
    half = cos_t.shape[0]
    zeros = jnp.zeros((LANE - half, cos_t.shape[1]), F32)
    cos = jnp.concatenate([cos_t, zeros], axis=0).T
    sin = jnp.concatenate([sin_t, zeros], axis=0).T
    rot = (pe * cos - pe_swapped * sin) + pltpu.roll(pe_swapped * cos + pe * sin, half, axis=1)
    kpe_ref[:, :LANE] = rot.astype(BF16)
    kpe_ref[:, LANE:] = pltpu.roll(rot, LANE // 2, axis=1).astype(BF16)
    kn_ref[...] = jnp.dot(c_kv, wkn_ref[...], preferred_element_type=F32).astype(BF16)
    vt = lax.dot_general(wvt_ref[...], c_kv, (((1,), (1,)), ((), ())), preferred_element_type=F32)
    vt_ref[0] = vt.astype(BF16)


def _kv_proj(x, norm, wa_ext, latent_norm, w_kn, w_vt, pos, inv_freq):
    tokens, d = x.shape
    tm = ATTN_TILE
    n_kn, n_v = w_kn.shape[1], w_vt.shape[0]
    return pl.pallas_call(
        _kv_kernel,
        name="kv_proj",
        grid=(tokens // tm,),
        in_specs=[
            pl.BlockSpec((tm, d), lambda i: (i, 0)),
            pl.BlockSpec((1, d), lambda i: (0, 0)),
            pl.BlockSpec(wa_ext.shape, lambda i: (0, 0)),
            pl.BlockSpec((1, KV_LORA_RANK), lambda i: (0, 0)),
            pl.BlockSpec(w_kn.shape, lambda i: (0, 0)),
            pl.BlockSpec(w_vt.shape, lambda i: (0, 0)),
            pl.BlockSpec((1, tm), lambda i: (0, i)),
            pl.BlockSpec(inv_freq.shape, lambda i: (0, 0)),
        ],
        out_specs=[
            pl.BlockSpec((tm, n_kn), lambda i: (i, 0)),
            pl.BlockSpec((1, n_v, tm), lambda i: (i, 0, 0)),
            pl.BlockSpec((tm, 2 * LANE), lambda i: (i, 0)),
        ],
        out_shape=[
            jax.ShapeDtypeStruct((tokens, n_kn), BF16),
            jax.ShapeDtypeStruct((tokens // tm, n_v, tm), BF16),
            jax.ShapeDtypeStruct((tokens, 2 * LANE), BF16),
        ],
        compiler_params=_params(("arbitrary",)),
    )(x, norm, wa_ext, latent_norm, w_kn, w_vt, pos, inv_freq)


def _q_kernel(x_ref, g_ref, wa_ref, gl_ref, wb_ref, pos_ref, invf_ref, qn_ref, qp_ref, *, n_nope, n_pe):
    h = _rms(x_ref[...], g_ref[...]).astype(BF16)
    cq = jnp.dot(h, wa_ref[...], preferred_element_type=F32)
    cqn = _rms(cq, gl_ref[...]).astype(BF16)
    qt = lax.dot_general(wb_ref[...], cqn, (((1,), (1,)), ((), ())), preferred_element_type=F32)
    qn_ref[...] = (qt[:n_nope, :] * Q_PRESCALE).astype(BF16)
    cos_t, sin_t = _rope_cos_sin(pos_ref, invf_ref)
    cos_t, sin_t = cos_t * Q_PRESCALE, sin_t * Q_PRESCALE
    reps = LANE // (2 * cos_t.shape[0])
    cos = jnp.concatenate([cos_t, cos_t] * reps, axis=0)
    sin = jnp.concatenate([-sin_t, sin_t] * reps, axis=0)
    for p in range(n_pe // LANE):
        pe = qt[n_nope + p * LANE:n_nope + (p + 1) * LANE, :]
        sw = qt[n_nope + n_pe + p * LANE:n_nope + n_pe + (p + 1) * LANE, :]
        qp_ref[p * LANE:(p + 1) * LANE, :] = (pe * cos + sw * sin).astype(BF16)


def _q_proj(x, norm, wa, latent_norm, wb_ext, pos, inv_freq):
    tokens, d = x.shape
    tm = ROW_TILE
    n_nope = N_HEADS * QK_NOPE_DIM
    n_pe = N_HEADS * QK_ROPE_DIM
    kern = functools.partial(_q_kernel, n_nope=n_nope, n_pe=n_pe)
    return pl.pallas_call(
        kern,
        name="q_proj",
        grid=(tokens // tm,),
        in_specs=[
            pl.BlockSpec((tm, d), lambda i: (i, 0)),
            pl.BlockSpec((1, d), lambda i: (0, 0)),
            pl.BlockSpec(wa.shape, lambda i: (0, 0)),
            pl.BlockSpec((1, wa.shape[1]), lambda i: (0, 0)),
            pl.BlockSpec(wb_ext.shape, lambda i: (0, 0)),
            pl.BlockSpec((1, tm), lambda i: (0, i)),
            pl.BlockSpec(inv_freq.shape, lambda i: (0, 0)),
        ],
        out_specs=[
            pl.BlockSpec((n_nope, tm), lambda i: (0, i)),
            pl.BlockSpec((n_pe, tm), lambda i: (0, i)),
        ],
        out_shape=[
            jax.ShapeDtypeStruct((n_nope, tokens), BF16),
            jax.ShapeDtypeStruct((n_pe, tokens), BF16),
        ],
        compiler_params=_params(("arbitrary",)),
    )(x, norm, wa, latent_norm, wb_ext, pos, inv_freq)


def _attn_kernel(qn_ref, qp_ref, kn_ref, kp_ref, vt_ref, o_ref, st_ref, acc_ref, *, blk, n_sub):
    qi = pl.program_id(2)
    base = n_sub * qi
    qp = qp_ref[...]
    qts = [jnp.concatenate([qn_ref[h * LANE:(h + 1) * LANE, :], qp], axis=0) for h in range(2)]
    acc_ref[...] = jnp.zeros(acc_ref.shape, F32)

    def scores(h, j, first_sub):
        off = pl.multiple_of(j * blk, blk)
        k = jnp.concatenate([kn_ref[pl.ds(off, blk), h * LANE:(h + 1) * LANE],
                             kp_ref[pl.ds(off, blk), h * LANE:(h + 1) * LANE]], axis=1)
        return jnp.dot(k, qts[h][:, first_sub * blk:], preferred_element_type=F32)

    ones_rows = jnp.ones((SUM_ROWS, blk), BF16)

    def consume(h, s, slot, vblk, m, cmax):
        m_new = jnp.maximum(m, cmax)
        alpha = jnp.exp2(m - m_new)
        pt = jnp.exp2(st_ref[h, s, slot] - m_new)
        vt = jnp.concatenate([vt_ref[vblk, h * V_DIM:(h + 1) * V_DIM, :], ones_rows], axis=0)
        acc_ref[h, s] = alpha * acc_ref[h, s] + jnp.dot(vt, pt.astype(BF16), preferred_element_type=F32)
        return m_new

    def colmax(st):
        return jnp.max(st, axis=0, keepdims=True)

    def step(t, carry, parity, first_sub=0, mask=None):
        carry = list(carry)
        for h in range(2):
            st = scores(h, t, first_sub)
            for s in range(max(first_sub - 1, 0), n_sub):
                m, cmax = carry[n_sub * h + s]
                m = consume(h, s, 1 - parity, t - 1, m, cmax)
                if s >= first_sub:
                    st_s = st[:, (s - first_sub) * blk:(s - first_sub + 1) * blk]
                    if s == first_sub and mask is not None:
                        st_s = jnp.where(mask, st_s, -jnp.inf)
                    st_ref[h, s, parity] = st_s
                    cmax = colmax(st_s)
                carry[n_sub * h + s] = (m, cmax)
        return tuple(carry)

    key = lax.broadcasted_iota(jnp.int32, (blk, blk), 0)
    qry = lax.broadcasted_iota(jnp.int32, (blk, blk), 1)
    causal = key <= qry
    m0 = jnp.full((1, blk), -jnp.inf, F32)

    first_mask = jnp.logical_or(causal, qi > 0)
    carry = []
    for h in range(2):
        st = scores(h, 0, 0)
        for s in range(n_sub):
            st_s = st[:, s * blk:(s + 1) * blk]
            if s == 0:
                st_s = jnp.where(first_mask, st_s, -jnp.inf)
            st_ref[h, s, 0] = st_s
            carry.append((m0, colmax(st_s)))

    def pair(u, carry):
        return step(2 * u + 2, step(2 * u + 1, carry, 1), 0)

    carry = lax.fori_loop(0, jnp.maximum(base // 2 - 1, 0), pair, tuple(carry))
    carry = lax.cond(qi >= 1,
                     lambda c: step(base, step(base - 1, c, 1), 0, 0, causal),
                     lambda c: c, carry)

    for k in range(1, n_sub):
        carry = step(base + k, carry, k % 2, k, causal)

    last = n_sub - 1
    for h in range(2):
        consume(h, last, last % 2, base + last, *carry[n_sub * h + last])
        for s in range(n_sub):
            out = acc_ref[h, s, :V_DIM, :] / acc_ref[h, s, V_DIM:V_DIM + 1, :]
            o_ref[s * blk:(s + 1) * blk, h * LANE:(h + 1) * LANE] = out.T.astype(BF16)


def _attention(qn, qp, kn, kpe, vt, batch, seq):
    tokens = qn.shape[1]
    blk, n_sub = ATTN_TILE, ATTN_SUB_TILES
    rows = n_sub * blk
    nk = seq // blk
    nq = seq // rows
    kern = functools.partial(_attn_kernel, blk=blk, n_sub=n_sub)
    return pl.pallas_call(
        kern,
        name="attention",
        grid=(batch, N_HEADS // 2, nq),
        in_specs=[
            pl.BlockSpec((2 * LANE, rows), lambda b, p, i: (p, b * nq + i)),
            pl.BlockSpec((LANE, rows), lambda b, p, i: (p, b * nq + i)),
            pl.BlockSpec((seq, 2 * LANE), lambda b, p, i: (b, p)),
            pl.BlockSpec((seq, 2 * LANE), lambda b, p, i: (b, 0)),
            pl.BlockSpec((nk, 2 * V_DIM, blk), lambda b, p, i: (b, p, 0)),
        ],
        out_specs=pl.BlockSpec((rows, 2 * LANE), lambda b, p, i: (b * nq + i, p)),
        out_shape=jax.ShapeDtypeStruct((tokens, N_HEADS * V_DIM), BF16),
        scratch_shapes=[pltpu.VMEM((2, n_sub, 2, blk, blk), F32),
                        pltpu.VMEM((2, n_sub, V_DIM + SUM_ROWS, blk), F32)],
        compiler_params=_params(("arbitrary", "arbitrary", "arbitrary")),
    )(qn, qp, kn, kpe, vt)


def _out_proj_kernel(x_ref, a_ref, w_ref, o_ref):
    o_ref[...] = x_ref[...] + jnp.dot(a_ref[...], w_ref[...], preferred_element_type=F32)


def _out_proj(x, attn, w_o):
    tokens, d = x.shape
    tm = ROW_TILE
    return pl.pallas_call(
        _out_proj_kernel,
        name="out_proj",
        grid=(tokens // tm,),
        in_specs=[
            pl.BlockSpec((tm, d), lambda i: (i, 0)),
            pl.BlockSpec((tm, attn.shape[1]), lambda i: (i, 0)),
            pl.BlockSpec(w_o.shape, lambda i: (0, 0)),
        ],
        out_specs=pl.BlockSpec((tm, d), lambda i: (i, 0)),
        out_shape=jax.ShapeDtypeStruct((tokens, d), F32),
        compiler_params=_params(("arbitrary",)),
    )(x, attn, w_o)


def _swap_halves(w):
    half = w.shape[-1] // 2
    return jnp.concatenate([w[..., half:], w[..., :half]], axis=-1)


def kernel(x, positions, pool_norm, pool_w, pool_scale, kv_in_norm, w_kv_a, kv_latent_norm, w_kv_b, attn_norm, w_q_a, q_latent_norm, w_q_b, w_o, ffn_norm, w_gate, w_up, w_down, final_norm):
    batch, seq, d = x.shape
    depth = ffn_norm.shape[0]
    n_pool = pool_norm.shape[0]
    assert seq % ROW_TILE == 0 and seq % (ATTN_SUB_TILES * ATTN_TILE) == 0 and ATTN_SUB_TILES % 2 == 0
    assert w_gate.shape[2] % FF_TILE == 0 and seq % FFN_ROW_TILE == 0

    half = QK_ROPE_DIM // 2
    inv_freq = (ROPE_BASE ** (-jnp.arange(half, dtype=F32) / half)).reshape(half, 1)
    pos = positions.astype(F32).reshape(1, batch * seq)
    row = lambda v: v.reshape(1, -1)
    xs = x.reshape(batch * seq, d)

    wg, wu, wd = w_gate.astype(BF16), w_up.astype(BF16), w_down.astype(BF16)
    zpad = jnp.zeros((d, LANE - QK_ROPE_DIM), F32)
    w_pe = w_kv_a[:, KV_LORA_RANK:]
    wa_ext = jnp.concatenate([w_kv_a[:, :KV_LORA_RANK], w_pe, zpad, _swap_halves(w_pe), zpad], axis=1).astype(BF16)
    wkb = w_kv_b.reshape(KV_LORA_RANK, N_HEADS, QK_NOPE_DIM + V_DIM)
    w_kn = wkb[:, :, :QK_NOPE_DIM].reshape(KV_LORA_RANK, -1).astype(BF16)
    w_vt = wkb[:, :, QK_NOPE_DIM:].reshape(KV_LORA_RANK, -1).T.astype(BF16)

    kn = vt = kpe = None
    for l in range(depth):
        if l == n_pool:
            kn, vt, kpe = _kv_proj(xs, row(kv_in_norm), wa_ext, row(kv_latent_norm), w_kn, w_vt, pos, inv_freq)
        if l < n_pool:
            xs = _pool_mixer(xs, seq, row(pool_norm[l]), pool_w[l].astype(BF16), row(pool_scale[l]))
        else:
            b = l - n_pool
            rank = w_q_b.shape[1]
            wqb = w_q_b[b].reshape(rank, N_HEADS, QK_DIM)
            wqb_pe = wqb[:, :, QK_NOPE_DIM:]
            wqb_ext = jnp.concatenate([
                wqb[:, :, :QK_NOPE_DIM].reshape(rank, -1),
                wqb_pe.reshape(rank, -1),
                _swap_halves(wqb_pe).reshape(rank, -1)], axis=1).T.astype(BF16)
            qn, qp = _q_proj(xs, row(attn_norm[b]), w_q_a[b].astype(BF16), row(q_latent_norm[b]), wqb_ext, pos, inv_freq)
            attn = _attention(qn, qp, kn, kpe, vt, batch, seq)
            xs = _out_proj(xs, attn, w_o[b].astype(BF16))
        last = l == depth - 1
        xs = _ffn(xs, row(ffn_norm[l]), wg, wu, wd, row(final_norm), layer=l, final_norm=last)
    return xs.reshape(batch, seq, d)
```

```python
import functools
import math

import jax
import jax.numpy as jnp
from jax import lax
from jax.experimental import pallas as pl
from jax.experimental.pallas import tpu as pltpu

N_HEADS = 16
QK_NOPE_DIM = 128
QK_ROPE_DIM = 64
QK_DIM = QK_NOPE_DIM + QK_ROPE_DIM
V_DIM = 128
KV_LORA_RANK = 512
POOL_WINDOWS = (2, 4, 8, 16)
ROPE_BASE = 10000.0
NORM_EPS = 1e-6

LANE = 128
SUBLANE = 8
assert POOL_WINDOWS == tuple(2 ** (k + 1) for k in range(len(POOL_WINDOWS)))
HALO = SUBLANE * len(POOL_WINDOWS)
SUM_ROWS = 16
Q_PRESCALE = (1.0 / math.sqrt(QK_DIM)) * math.log2(math.e)

ROW_TILE = 512
FF_TILE = 512
FFN_ROW_TILE = 1024
FFN_SUB_ROWS = 512
ATTN_TILE = 512
ATTN_SUB_TILES = 2
VMEM_LIMIT = 56 * 1024 * 1024
FFN_VMEM_LIMIT = 60 * 1024 * 1024

F32 = jnp.float32
BF16 = jnp.bfloat16


def _rms(x, g):
    ms = jnp.mean(x * x, axis=-1, keepdims=True)
    return x * lax.rsqrt(ms + NORM_EPS) * g


def _rope_cos_sin(pos_ref, invf_ref):
    ang = invf_ref[...] * pos_ref[...]
    return jnp.cos(ang), jnp.sin(ang)


def _params(semantics, flags=None):
    return pltpu.CompilerParams(dimension_semantics=semantics, vmem_limit_bytes=VMEM_LIMIT, flags=flags)


def _pool_kernel(x_ref, halo_ref, g_ref, w_ref, sc_ref, o_ref, hh_ref, lv_ref, *, tiles_per_seq, tm, pg):
    t = pl.program_id(0) % tiles_per_seq
    g = g_ref[...]
    rows_end = HALO + tm
    hh_ref[HALO:, :] = _rms(x_ref[...], g)
    hh_ref[:HALO, :] = jnp.where(t == 0, 0.0, _rms(halo_ref[...], g))
    for k in range(1, len(POOL_WINDOWS) + 1):
        start, shift, cols = SUBLANE * k, 2 ** (k - 1), slice((k - 1) * pg, None)
        src = hh_ref if k == 1 else lv_ref.at[k % 2]
        lv_ref[(k - 1) % 2, start:rows_end, cols] = (src[start:rows_end, cols]
                                                     + src[start - shift:rows_end - shift, cols])
    pos = t * tm + lax.broadcasted_iota(jnp.int32, (tm, 1), 0)
    for gi, w in enumerate(POOL_WINDOWS):
        cols = slice(gi * pg, (gi + 1) * pg)
        count = jnp.minimum(pos + 1, w).astype(F32)
        diff = lv_ref[gi % 2, HALO:, cols] / count - hh_ref[HALO:, cols]
        y = jnp.dot(diff.astype(BF16), w_ref[gi], preferred_element_type=F32)
        o_ref[:, cols] = x_ref[:, cols] + y * sc_ref[:, cols]


def _pool_mixer(x, seq, norm, w, scale):
    tokens, d = x.shape
    tm = ROW_TILE
    pg = d // len(POOL_WINDOWS)
    kern = functools.partial(_pool_kernel, tiles_per_seq=seq // tm, tm=tm, pg=pg)
    return pl.pallas_call(
        kern,
        name="pool_mixer",
        grid=(tokens // tm,),
        in_specs=[
            pl.BlockSpec((tm, d), lambda i: (i, 0)),
            pl.BlockSpec((HALO, d), lambda i: (jnp.maximum(i * (tm // HALO) - 1, 0), 0)),
            pl.BlockSpec((1, d), lambda i: (0, 0)),
            pl.BlockSpec(w.shape, lambda i: (0, 0, 0)),
            pl.BlockSpec((1, d), lambda i: (0, 0)),
        ],
        out_specs=pl.BlockSpec((tm, d), lambda i: (i, 0)),
        out_shape=jax.ShapeDtypeStruct((tokens, d), F32),
        scratch_shapes=[pltpu.VMEM((HALO + tm, d), F32), pltpu.VMEM((2, HALO + tm, d), F32)],
        compiler_params=_params(("arbitrary",)),
    )(x, x, norm, w, scale)


def _ffn_kernel(x_ref, g_ref, wg_ref, wu_ref, wd_ref, fg_ref, o_ref, hn_ref, *, final_norm):
    j = pl.program_id(1)

    halves = [slice(r * FFN_SUB_ROWS, (r + 1) * FFN_SUB_ROWS) for r in range(x_ref.shape[0] // FFN_SUB_ROWS)]

    def chunk(first):
        for rows in halves:
            if first:
                hn = _rms(x_ref[rows, :], g_ref[...]).astype(BF16)
                hn_ref[rows, :] = hn
            else:
                hn = hn_ref[rows, :]
            gate = jnp.dot(hn, wg_ref[...].astype(BF16), preferred_element_type=F32)
            up = jnp.dot(hn, wu_ref[...], preferred_element_type=F32)
            act = gate / (1.0 + jnp.exp(-gate)) * up
            down = jnp.dot(act.astype(BF16), wd_ref[...].astype(BF16), preferred_element_type=F32)
            o_ref[rows, :] = (x_ref[rows, :] if first else o_ref[rows, :]) + down

    pl.when(j == 0)(functools.partial(chunk, True))
    pl.when(j > 0)(functools.partial(chunk, False))

    if final_norm:
        @pl.when(j == pl.num_programs(1) - 1)
        def _():
            for rows in halves:
                o_ref[rows, :] = _rms(o_ref[rows, :], fg_ref[...])


def _ffn(x, norm, w_gate, w_up, w_down, final_g, *, layer, final_norm):
    tokens, d = x.shape
    dff = w_gate.shape[2]
    tm, tf = FFN_ROW_TILE, FF_TILE
    kern = functools.partial(_ffn_kernel, final_norm=final_norm)
    return pl.pallas_call(
        kern,
        name="ffn_final" if final_norm else "ffn",
        grid=(tokens // tm, dff // tf),
        in_specs=[
            pl.BlockSpec((tm, d), lambda i, j: (i, 0)),
            pl.BlockSpec((1, d), lambda i, j: (0, 0)),
            pl.BlockSpec((None, d, tf), lambda i, j: (layer, 0, j)),
            pl.BlockSpec((None, d, tf), lambda i, j: (layer, 0, j)),
            pl.BlockSpec((None, tf, d), lambda i, j: (layer, j, 0)),
            pl.BlockSpec((1, d), lambda i, j: (0, 0)),
        ],
        out_specs=pl.BlockSpec((tm, d), lambda i, j: (i, 0)),
        out_shape=jax.ShapeDtypeStruct((tokens, d), F32),
        scratch_shapes=[pltpu.VMEM((tm, d), BF16)],
        compiler_params=pltpu.CompilerParams(dimension_semantics=("arbitrary", "arbitrary"),
                                             vmem_limit_bytes=FFN_VMEM_LIMIT),
    )(x, norm, w_gate, w_up, w_down, final_g)


def _kv_kernel(x_ref, g_ref, wa_ref, gl_ref, wkn_ref, wvt_ref, pos_ref, invf_ref, kn_ref, vt_ref, kpe_ref):
    h = _rms(x_ref[...], g_ref[...]).astype(BF16)
    kv = jnp.dot(h, wa_ref[...], preferred_element_type=F32)
    c_kv = _rms(kv[:, :KV_LORA_RANK], gl_ref[...]).astype(BF16)
    pe = kv[:, KV_LORA_RANK:KV_LORA_RANK + LANE]
    pe_swapped = kv[:, KV_LORA_RANK + LANE:]
    cos_t, sin_t = _rope_cos_sin(pos_ref, invf_ref)
    half = cos_t.shape[0]
    zeros = jnp.zeros((LANE - half, cos_t.shape[1]), F32)
    cos = jnp.concatenate([cos_t, zeros], axis=0).T
    sin = jnp.concatenate([sin_t, zeros], axis=0).T
    rot = (pe * cos - pe_swapped * sin) + pltpu.roll(pe_swapped * cos + pe * sin, half, axis=1)
    kpe_ref[:, :LANE] = rot.astype(BF16)
    kpe_ref[:, LANE:] = pltpu.roll(rot, LANE // 2, axis=1).astype(BF16)
    kn_ref[...] = jnp.dot(c_kv, wkn_ref[...], preferred_element_type=F32).astype(BF16)
    vt = lax.dot_general(wvt_ref[...], c_kv, (((1,), (1,)), ((), ())), preferred_element_type=F32)
    vt_ref[0] = vt.astype(BF16)


def _kv_proj(x, norm, wa_ext, latent_norm, w_kn, w_vt, pos, inv_freq):
    tokens, d = x.shape
    tm = ATTN_TILE
    n_kn, n_v = w_kn.shape[1], w_vt.shape[0]
    return pl.pallas_call(
        _kv_kernel,
        name="kv_proj",
        grid=(tokens // tm,),
        in_specs=[
            pl.BlockSpec((tm, d), lambda i: (i, 0)),
            pl.BlockSpec((1, d), lambda i: (0, 0)),
            pl.BlockSpec(wa_ext.shape, lambda i: (0, 0)),
            pl.BlockSpec((1, KV_LORA_RANK), lambda i: (0, 0)),
            pl.BlockSpec(w_kn.shape, lambda i: (0, 0)),
            pl.BlockSpec(w_vt.shape, lambda i: (0, 0)),
            pl.BlockSpec((1, tm), lambda i: (0, i)),
            pl.BlockSpec(inv_freq.shape, lambda i: (0, 0)),
        ],
        out_specs=[
            pl.BlockSpec((tm, n_kn), lambda i: (i, 0)),
            pl.BlockSpec((1, n_v, tm), lambda i: (i, 0, 0)),
            pl.BlockSpec((tm, 2 * LANE), lambda i: (i, 0)),
        ],
        out_shape=[
            jax.ShapeDtypeStruct((tokens, n_kn), BF16),
            jax.ShapeDtypeStruct((tokens // tm, n_v, tm), BF16),
            jax.ShapeDtypeStruct((tokens, 2 * LANE), BF16),
        ],
        compiler_params=_params(("arbitrary",)),
    )(x, norm, wa_ext, latent_norm, w_kn, w_vt, pos, inv_freq)


def _q_kernel(x_ref, g_ref, wa_ref, gl_ref, wb_ref, pos_ref, invf_ref, qn_ref, qp_ref, *, n_nope, n_pe):
    h = _rms(x_ref[...], g_ref[...]).astype(BF16)
    cq = jnp.dot(h, wa_ref[...], preferred_element_type=F32)
    cqn = _rms(cq, gl_ref[...]).astype(BF16)
    qt = lax.dot_general(wb_ref[...], cqn, (((1,), (1,)), ((), ())), preferred_element_type=F32)
    qn_ref[...] = (qt[:n_nope, :] * Q_PRESCALE).astype(BF16)
    cos_t, sin_t = _rope_cos_sin(pos_ref, invf_ref)
    cos_t, sin_t = cos_t * Q_PRESCALE, sin_t * Q_PRESCALE
    reps = LANE // (2 * cos_t.shape[0])
    cos = jnp.concatenate([cos_t, cos_t] * reps, axis=0)
    sin = jnp.concatenate([-sin_t, sin_t] * reps, axis=0)
    for p in range(n_pe // LANE):
        pe = qt[n_nope + p * LANE:n_nope + (p + 1) * LANE, :]
        sw = qt[n_nope + n_pe + p * LANE:n_nope + n_pe + (p + 1) * LANE, :]
        qp_ref[p * LANE:(p + 1) * LANE, :] = (pe * cos + sw * sin).astype(BF16)


def _q_proj(x, norm, wa, latent_norm, wb_ext, pos, inv_freq):
    tokens, d = x.shape
    tm = ROW_TILE
    n_nope = N_HEADS * QK_NOPE_DIM
    n_pe = N_HEADS * QK_ROPE_DIM
    kern = functools.partial(_q_kernel, n_nope=n_nope, n_pe=n_pe)
    return pl.pallas_call(
        kern,
        name="q_proj",
        grid=(tokens // tm,),
        in_specs=[
            pl.BlockSpec((tm, d), lambda i: (i, 0)),
            pl.BlockSpec((1, d), lambda i: (0, 0)),
            pl.BlockSpec(wa.shape, lambda i: (0, 0)),
            pl.BlockSpec((1, wa.shape[1]), lambda i: (0, 0)),
            pl.BlockSpec(wb_ext.shape, lambda i: (0, 0)),
            pl.BlockSpec((1, tm), lambda i: (0, i)),
            pl.BlockSpec(inv_freq.shape, lambda i: (0, 0)),
        ],
        out_specs=[
            pl.BlockSpec((n_nope, tm), lambda i: (0, i)),
            pl.BlockSpec((n_pe, tm), lambda i: (0, i)),
        ],
        out_shape=[
            jax.ShapeDtypeStruct((n_nope, tokens), BF16),
            jax.ShapeDtypeStruct((n_pe, tokens), BF16),
        ],
        compiler_params=_params(("arbitrary",)),
    )(x, norm, wa, latent_norm, wb_ext, pos, inv_freq)


def _attn_kernel(qn_ref, qp_ref, kn_ref, kp_ref, vt_ref, o_ref, st_ref, acc_ref, *, blk, n_sub):
    qi = pl.program_id(2)
    base = n_sub * qi
    qp = qp_ref[...]
    qts = [jnp.concatenate([qn_ref[h * LANE:(h + 1) * LANE, :], qp], axis=0) for h in range(2)]
    acc_ref[...] = jnp.zeros(acc_ref.shape, F32)

    def scores(h, j, first_sub):
        off = pl.multiple_of(j * blk, blk)
        k = jnp.concatenate([kn_ref[pl.ds(off, blk), h * LANE:(h + 1) * LANE],
                             kp_ref[pl.ds(off, blk), h * LANE:(h + 1) * LANE]], axis=1)
        return jnp.dot(k, qts[h][:, first_sub * blk:], preferred_element_type=F32)

    ones_rows = jnp.ones((SUM_ROWS, blk), BF16)

    def consume(h, s, slot, vblk, m, cmax):
        m_new = jnp.maximum(m, cmax)
        alpha = jnp.exp2(m - m_new)
        pt = jnp.exp2(st_ref[h, s, slot] - m_new)
        vt = jnp.concatenate([vt_ref[vblk, h * V_DIM:(h + 1) * V_DIM, :], ones_rows], axis=0)
        acc_ref[h, s] = alpha * acc_ref[h, s] + jnp.dot(vt, pt.astype(BF16), preferred_element_type=F32)
        return m_new

    def colmax(st):
        return jnp.max(st, axis=0, keepdims=True)

    def step(t, carry, parity, first_sub=0, mask=None):
        carry = list(carry)
        for h in range(2):
            st = scores(h, t, first_sub)
            for s in range(max(first_sub - 1, 0), n_sub):
                m, cmax = carry[n_sub * h + s]
                m = consume(h, s, 1 - parity, t - 1, m, cmax)
                if s >= first_sub:
                    st_s = st[:, (s - first_sub) * blk:(s - first_sub + 1) * blk]
                    if s == first_sub and mask is not None:
                        st_s = jnp.where(mask, st_s, -jnp.inf)
                    st_ref[h, s, parity] = st_s
                    cmax = colmax(st_s)
                carry[n_sub * h + s] = (m, cmax)
        return tuple(carry)

    key = lax.broadcasted_iota(jnp.int32, (blk, blk), 0)
    qry = lax.broadcasted_iota(jnp.int32, (blk, blk), 1)
    causal = key <= qry
    m0 = jnp.full((1, blk), -jnp.inf, F32)

    first_mask = jnp.logical_or(causal, qi > 0)
    carry = []
    for h in range(2):
        st = scores(h, 0, 0)
        for s in range(n_sub):
            st_s = st[:, s * blk:(s + 1) * blk]
            if s == 0:
                st_s = jnp.where(first_mask, st_s, -jnp.inf)
            st_ref[h, s, 0] = st_s
            carry.append((m0, colmax(st_s)))

    def pair(u, carry):
        return step(2 * u + 2, step(2 * u + 1, carry, 1), 0)

    carry = lax.fori_loop(0, jnp.maximum(base // 2 - 1, 0), pair, tuple(carry))
    carry = lax.cond(qi >= 1,
                     lambda c: step(base, step(base - 1, c, 1), 0, 0, causal),
                     lambda c: c, carry)

    for k in range(1, n_sub):
        carry = step(base + k, carry, k % 2, k, causal)

    last = n_sub - 1
    for h in range(2):
        consume(h, last, last % 2, base + last, *carry[n_sub * h + last])
        for s in range(n_sub):
            out = acc_ref[h, s, :V_DIM, :] / acc_ref[h, s, V_DIM:V_DIM + 1, :]
            o_ref[s * blk:(s + 1) * blk, h * LANE:(h + 1) * LANE] = out.T.astype(BF16)


def _attention(qn, qp, kn, kpe, vt, batch, seq):
    tokens = qn.shape[1]
    blk, n_sub = ATTN_TILE, ATTN_SUB_TILES
    rows = n_sub * blk
    nk = seq // blk
    nq = seq // rows
    kern = functools.partial(_attn_kernel, blk=blk, n_sub=n_sub)
    return pl.pallas_call(
        kern,
        name="attention",
        grid=(batch, N_HEADS // 2, nq),
        in_specs=[
            pl.BlockSpec((2 * LANE, rows), lambda b, p, i: (p, b * nq + i)),
            pl.BlockSpec((LANE, rows), lambda b, p, i: (p, b * nq + i)),
            pl.BlockSpec((seq, 2 * LANE), lambda b, p, i: (b, p)),
            pl.BlockSpec((seq, 2 * LANE), lambda b, p, i: (b, 0)),
            pl.BlockSpec((nk, 2 * V_DIM, blk), lambda b, p, i: (b, p, 0)),
        ],
        out_specs=pl.BlockSpec((rows, 2 * LANE), lambda b, p, i: (b * nq + i, p)),
        out_shape=jax.ShapeDtypeStruct((tokens, N_HEADS * V_DIM), BF16),
        scratch_shapes=[pltpu.VMEM((2, n_sub, 2, blk, blk), F32),
                        pltpu.VMEM((2, n_sub, V_DIM + SUM_ROWS, blk), F32)],
        compiler_params=_params(("arbitrary", "arbitrary", "arbitrary")),
    )(qn, qp, kn, kpe, vt)


def _out_proj_kernel(x_ref, a_ref, w_ref, o_ref):
    o_ref[...] = x_ref[...] + jnp.dot(a_ref[...], w_ref[...], preferred_element_type=F32)


def _out_proj(x, attn, w_o):
    tokens, d = x.shape
    tm = ROW_TILE
    return pl.pallas_call(
        _out_proj_kernel,
        name="out_proj",
        grid=(tokens // tm,),
        in_specs=[
            pl.BlockSpec((tm, d), lambda i: (i, 0)),
            pl.BlockSpec((tm, attn.shape[1]), lambda i: (i, 0)),
            pl.BlockSpec(w_o.shape, lambda i: (0, 0)),
        ],
        out_specs=pl.BlockSpec((tm, d), lambda i: (i, 0)),
        out_shape=jax.ShapeDtypeStruct((tokens, d), F32),
        compiler_params=_params(("arbitrary",)),
    )(x, attn, w_o)


def _swap_halves(w):
    half = w.shape[-1] // 2
    return jnp.concatenate([w[..., half:], w[..., :half]], axis=-1)


def kernel(x, positions, pool_norm, pool_w, pool_scale, kv_in_norm, w_kv_a, kv_latent_norm, w_kv_b, attn_norm, w_q_a, q_latent_norm, w_q_b, w_o, ffn_norm, w_gate, w_up, w_down, final_norm):
    batch, seq, d = x.shape
    depth = ffn_norm.shape[0]
    n_pool = pool_norm.shape[0]
    assert seq % ROW_TILE == 0 and seq % (ATTN_SUB_TILES * ATTN_TILE) == 0 and ATTN_SUB_TILES % 2 == 0
    assert w_gate.shape[2] % FF_TILE == 0 and seq % FFN_ROW_TILE == 0

    half = QK_ROPE_DIM // 2
    inv_freq = (ROPE_BASE ** (-jnp.arange(half, dtype=F32) / half)).reshape(half, 1)
    pos = positions.astype(F32).reshape(1, batch * seq)
    row = lambda v: v.reshape(1, -1)
    xs = x.reshape(batch * seq, d)

    wg, wu, wd = w_gate, w_up.astype(BF16), w_down
    zpad = jnp.zeros((d, LANE - QK_ROPE_DIM), F32)
    w_pe = w_kv_a[:, KV_LORA_RANK:]
    wa_ext = jnp.concatenate([w_kv_a[:, :KV_LORA_RANK], w_pe, zpad, _swap_halves(w_pe), zpad], axis=1).astype(BF16)
    wkb = w_kv_b.reshape(KV_LORA_RANK, N_HEADS, QK_NOPE_DIM + V_DIM)
    w_kn = wkb[:, :, :QK_NOPE_DIM].reshape(KV_LORA_RANK, -1).astype(BF16)
    w_vt = wkb[:, :, QK_NOPE_DIM:].reshape(KV_LORA_RANK, -1).T.astype(BF16)

    kn = vt = kpe = None
    for l in range(depth):
        if l == n_pool:
            kn, vt, kpe = _kv_proj(xs, row(kv_in_norm), wa_ext, row(kv_latent_norm), w_kn, w_vt, pos, inv_freq)
        if l < n_pool:
            xs = _pool_mixer(xs, seq, row(pool_norm[l]), pool_w[l].astype(BF16), row(pool_scale[l]))
        else:
            b = l - n_pool
            rank = w_q_b.shape[1]
            wqb = w_q_b[b].reshape(rank, N_HEADS, QK_DIM)
            wqb_pe = wqb[:, :, QK_NOPE_DIM:]
            wqb_ext = jnp.concatenate([
                wqb[:, :, :QK_NOPE_DIM].reshape(rank, -1),
                wqb_pe.reshape(rank, -1),
                _swap_halves(wqb_pe).reshape(rank, -1)], axis=1).T.astype(BF16)
            qn, qp = _q_proj(xs, row(attn_norm[b]), w_q_a[b].astype(BF16), row(q_latent_norm[b]), wqb_ext, pos, inv_freq)
            attn = _attention(qn, qp, kn, kpe, vt, batch, seq)
            xs = _out_proj(xs, attn, w_o[b].astype(BF16))
        last = l == depth - 1
        xs = _ffn(xs, row(ffn_norm[l]), wg, wu, wd, row(final_norm), layer=l, final_norm=last)
    return xs.reshape(batch, seq, d)
```

```python
import functools
import math

import jax
import jax.numpy as jnp
from jax import lax
from jax.experimental import pallas as pl
from jax.experimental.pallas import tpu as pltpu

N_HEADS = 16
QK_NOPE_DIM = 128
QK_ROPE_DIM = 64
QK_DIM = QK_NOPE_DIM + QK_ROPE_DIM
V_DIM = 128
KV_LORA_RANK = 512
POOL_WINDOWS = (2, 4, 8, 16)
ROPE_BASE = 10000.0
NORM_EPS = 1e-6

LANE = 128
SUBLANE = 8
assert POOL_WINDOWS == tuple(2 ** (k + 1) for k in range(len(POOL_WINDOWS)))
HALO = SUBLANE * len(POOL_WINDOWS)
SUM_ROWS = 16
Q_PRESCALE = (1.0 / math.sqrt(QK_DIM)) * math.log2(math.e)

ROW_TILE = 512
FF_TILE = 512
FFN_ROW_TILE = 1024
FFN_SUB_ROWS = 512
ATTN_TILE = 512
ATTN_SUB_TILES = 2
VMEM_LIMIT = 56 * 1024 * 1024
FFN_VMEM_LIMIT = 60 * 1024 * 1024

F32 = jnp.float32
BF16 = jnp.bfloat16


def _rms(x, g):
    ms = jnp.mean(x * x, axis=-1, keepdims=True)
    return x * lax.rsqrt(ms + NORM_EPS) * g


def _rope_cos_sin(pos_ref, invf_ref):
    ang = invf_ref[...] * pos_ref[...]
    return jnp.cos(ang), jnp.sin(ang)


def _params(semantics, flags=None):
    return pltpu.CompilerParams(dimension_semantics=semantics, vmem_limit_bytes=VMEM_LIMIT, flags=flags)


def _pool_kernel(x_ref, halo_ref, g_ref, w_ref, sc_ref, o_ref, hh_ref, lv_ref, *, tiles_per_seq, tm, pg):
    t = pl.program_id(0) % tiles_per_seq
    g = g_ref[...]
    rows_end = HALO + tm
    hh_ref[HALO:, :] = _rms(x_ref[...], g)
    hh_ref[:HALO, :] = jnp.where(t == 0, 0.0, _rms(halo_ref[...], g))
    for k in range(1, len(POOL_WINDOWS) + 1):
        start, shift, cols = SUBLANE * k, 2 ** (k - 1), slice((k - 1) * pg, None)
        src = hh_ref if k == 1 else lv_ref.at[k % 2]
        lv_ref[(k - 1) % 2, start:rows_end, cols] = (src[start:rows_end, cols]
                                                     + src[start - shift:rows_end - shift, cols])
    pos = t * tm + lax.broadcasted_iota(jnp.int32, (tm, 1), 0)
    for gi, w in enumerate(POOL_WINDOWS):
        cols = slice(gi * pg, (gi + 1) * pg)
        count = jnp.minimum(pos + 1, w).astype(F32)
        diff = lv_ref[gi % 2, HALO:, cols] / count - hh_ref[HALO:, cols]
        y = jnp.dot(diff.astype(BF16), w_ref[gi], preferred_element_type=F32)
        o_ref[:, cols] = x_ref[:, cols] + y * sc_ref[:, cols]


def _pool_mixer(x, seq, norm, w, scale):
    tokens, d = x.shape
    tm = ROW_TILE
    pg = d // len(POOL_WINDOWS)
    kern = functools.partial(_pool_kernel, tiles_per_seq=seq // tm, tm=tm, pg=pg)
    return pl.pallas_call(
        kern,
        name="pool_mixer",
        grid=(tokens // tm,),
        in_specs=[
            pl.BlockSpec((tm, d), lambda i: (i, 0)),
            pl.BlockSpec((HALO, d), lambda i: (jnp.maximum(i * (tm // HALO) - 1, 0), 0)),
            pl.BlockSpec((1, d), lambda i: (0, 0)),
            pl.BlockSpec(w.shape, lambda i: (0, 0, 0)),
            pl.BlockSpec((1, d), lambda i: (0, 0)),
        ],
        out_specs=pl.BlockSpec((tm, d), lambda i: (i, 0)),
        out_shape=jax.ShapeDtypeStruct((tokens, d), F32),
        scratch_shapes=[pltpu.VMEM((HALO + tm, d), F32), pltpu.VMEM((2, HALO + tm, d), F32)],
        compiler_params=_params(("arbitrary",)),
    )(x, x, norm, w, scale)


def _ffn_kernel(x_ref, g_ref, wg_ref, wu_ref, wd_ref, fg_ref, o_ref, hn_ref, *, final_norm):
    j = pl.program_id(1)

    halves = [slice(r * FFN_SUB_ROWS, (r + 1) * FFN_SUB_ROWS) for r in range(x_ref.shape[0] // FFN_SUB_ROWS)]

    def chunk(first):
        for rows in halves:
            if first:
                hn = _rms(x_ref[rows, :], g_ref[...]).astype(BF16)
                hn_ref[rows, :] = hn
            else:
                hn = hn_ref[rows, :]
            gate = jnp.dot(hn, wg_ref[...].astype(BF16), preferred_element_type=F32)
            up = jnp.dot(hn, wu_ref[...], preferred_element_type=F32)
            act = gate / (1.0 + jnp.exp(-gate)) * up
            down = jnp.dot(act.astype(BF16), wd_ref[...].astype(BF16), preferred_element_type=F32)
            o_ref[rows, :] = (x_ref[rows, :] if first else o_ref[rows, :]) + down

    pl.when(j == 0)(functools.partial(chunk, True))
    pl.when(j > 0)(functools.partial(chunk, False))

    if final_norm:
        @pl.when(j == pl.num_programs(1) - 1)
        def _():
            for rows in halves:
                o_ref[rows, :] = _rms(o_ref[rows, :], fg_ref[...])


def _ffn(x, norm, w_gate, w_up, w_down, final_g, *, layer, final_norm):
    tokens, d = x.shape
    dff = w_gate.shape[2]
    tm, tf = FFN_ROW_TILE, FF_TILE
    kern = functools.partial(_ffn_kernel, final_norm=final_norm)
    return pl.pallas_call(
        kern,
        name="ffn_final" if final_norm else "ffn",
        grid=(tokens // tm, dff // tf),
        in_specs=[
            pl.BlockSpec((tm, d), lambda i, j: (i, 0)),
            pl.BlockSpec((1, d), lambda i, j: (0, 0)),
            pl.BlockSpec((None, d, tf), lambda i, j: (layer, 0, j)),
            pl.BlockSpec((None, d, tf), lambda i, j: (layer, 0, j)),
            pl.BlockSpec((None, tf, d), lambda i, j: (layer, j, 0)),
            pl.BlockSpec((1, d), lambda i, j: (0, 0)),
        ],
        out_specs=pl.BlockSpec((tm, d), lambda i, j: (i, 0)),
        out_shape=jax.ShapeDtypeStruct((tokens, d), F32),
        scratch_shapes=[pltpu.VMEM((tm, d), BF16)],
        compiler_params=pltpu.CompilerParams(dimension_semantics=("arbitrary", "arbitrary"),
                                             vmem_limit_bytes=FFN_VMEM_LIMIT),
    )(x, norm, w_gate, w_up, w_down, final_g)


def _kv_kernel(x_ref, g_ref, wa_ref, gl_ref, wkn_ref, wvt_ref, pos_ref, invf_ref, kn_ref, vt_ref, kpe_ref):
    h = _rms(x_ref[...], g_ref[...]).astype(BF16)
    kv = jnp.dot(h, wa_ref[...], preferred_element_type=F32)
    c_kv = _rms(kv[:, :KV_LORA_RANK], gl_ref[...]).astype(BF16)
    pe = kv[:, KV_LORA_RANK:KV_LORA_RANK + LANE]
    pe_swapped = kv[:, KV_LORA_RANK + LANE:]
    cos_t, sin_t = _rope_cos_sin(pos_ref, invf_ref)
    half = cos_t.shape[0]
    zeros = jnp.zeros((LANE - half, cos_t.shape[1]), F32)
    cos = jnp.concatenate([cos_t, zeros], axis=0).T
    sin = jnp.concatenate([sin_t, zeros], axis=0).T
    rot = (pe * cos - pe_swapped * sin) + pltpu.roll(pe_swapped * cos + pe * sin, half, axis=1)
    kpe_ref[:, :LANE] = rot.astype(BF16)
    kpe_ref[:, LANE:] = pltpu.roll(rot, LANE // 2, axis=1).astype(BF16)
    kn_ref[...] = jnp.dot(c_kv, wkn_ref[...], preferred_element_type=F32).astype(BF16)
    vt = lax.dot_general(wvt_ref[...], c_kv, (((1,), (1,)), ((), ())), preferred_element_type=F32)
    vt_ref[0] = vt.astype(BF16)


def _kv_proj(x, norm, wa_ext, latent_norm, w_kn, w_vt, pos, inv_freq):
    tokens, d = x.shape
    tm = ATTN_TILE
    n_kn, n_v = w_kn.shape[1], w_vt.shape[0]
    return pl.pallas_call(
        _kv_kernel,
        name="kv_proj",
        grid=(tokens // tm,),
        in_specs=[
            pl.BlockSpec((tm, d), lambda i: (i, 0)),
            pl.BlockSpec((1, d), lambda i: (0, 0)),
            pl.BlockSpec(wa_ext.shape, lambda i: (0, 0)),
            pl.BlockSpec((1, KV_LORA_RANK), lambda i: (0, 0)),
            pl.BlockSpec(w_kn.shape, lambda i: (0, 0)),
            pl.BlockSpec(w_vt.shape, lambda i: (0, 0)),
            pl.BlockSpec((1, tm), lambda i: (0, i)),
            pl.BlockSpec(inv_freq.shape, lambda i: (0, 0)),
        ],
        out_specs=[
            pl.BlockSpec((tm, n_kn), lambda i: (i, 0)),
            pl.BlockSpec((1, n_v, tm), lambda i: (i, 0, 0)),
            pl.BlockSpec((tm, 2 * LANE), lambda i: (i, 0)),
        ],
        out_shape=[
            jax.ShapeDtypeStruct((tokens, n_kn), BF16),
            jax.ShapeDtypeStruct((tokens // tm, n_v, tm), BF16),
            jax.ShapeDtypeStruct((tokens, 2 * LANE), BF16),
        ],
        compiler_params=_params(("arbitrary",)),
    )(x, norm, wa_ext, latent_norm, w_kn, w_vt, pos, inv_freq)


def _q_kernel(x_ref, g_ref, wa_ref, gl_ref, wb_ref, pos_ref, invf_ref, qn_ref, qp_ref, *, n_nope, n_pe):
    h = _rms(x_ref[...], g_ref[...]).astype(BF16)
    cq = jnp.dot(h, wa_ref[...], preferred_element_type=F32)
    cqn = _rms(cq, gl_ref[...]).astype(BF16)
    qt = lax.dot_general(wb_ref[...], cqn, (((1,), (1,)), ((), ())), preferred_element_type=F32)
    qn_ref[...] = (qt[:n_nope, :] * Q_PRESCALE).astype(BF16)
    cos_t, sin_t = _rope_cos_sin(pos_ref, invf_ref)
    cos_t, sin_t = cos_t * Q_PRESCALE, sin_t * Q_PRESCALE
    half = cos_t.shape[0]
    reps = LANE // (2 * half)
    cos = jnp.concatenate([cos_t, cos_t] * reps, axis=0)
    sin = jnp.concatenate([-sin_t, sin_t] * reps, axis=0)
    for p in range(n_pe // LANE):
        pe = qt[n_nope + p * LANE:n_nope + (p + 1) * LANE, :]
        sw = jnp.concatenate([pe[(2 * r + 1 - c) * half:(2 * r + 2 - c) * half, :]
                              for r in range(reps) for c in range(2)], axis=0)
        qp_ref[p * LANE:(p + 1) * LANE, :] = (pe * cos + sw * sin).astype(BF16)


def _q_proj(x, norm, wa, latent_norm, wb_ext, pos, inv_freq):
    tokens, d = x.shape
    tm = ROW_TILE
    n_nope = N_HEADS * QK_NOPE_DIM
    n_pe = N_HEADS * QK_ROPE_DIM
    kern = functools.partial(_q_kernel, n_nope=n_nope, n_pe=n_pe)
    return pl.pallas_call(
        kern,
        name="q_proj",
        grid=(tokens // tm,),
        in_specs=[
            pl.BlockSpec((tm, d), lambda i: (i, 0)),
            pl.BlockSpec((1, d), lambda i: (0, 0)),
            pl.BlockSpec(wa.shape, lambda i: (0, 0)),
            pl.BlockSpec((1, wa.shape[1]), lambda i: (0, 0)),
            pl.BlockSpec(wb_ext.shape, lambda i: (0, 0)),
            pl.BlockSpec((1, tm), lambda i: (0, i)),
            pl.BlockSpec(inv_freq.shape, lambda i: (0, 0)),
        ],
        out_specs=[
            pl.BlockSpec((n_nope, tm), lambda i: (0, i)),
            pl.BlockSpec((n_pe, tm), lambda i: (0, i)),
        ],
        out_shape=[
            jax.ShapeDtypeStruct((n_nope, tokens), BF16),
            jax.ShapeDtypeStruct((n_pe, tokens), BF16),
        ],
        compiler_params=_params(("arbitrary",)),
    )(x, norm, wa, latent_norm, wb_ext, pos, inv_freq)


def _attn_kernel(qn_ref, qp_ref, kn_ref, kp_ref, vt_ref, o_ref, st_ref, acc_ref, *, blk, n_sub):
    qi = pl.program_id(2)
    base = n_sub * qi
    qp = qp_ref[...]
    qts = [jnp.concatenate([qn_ref[h * LANE:(h + 1) * LANE, :], qp], axis=0) for h in range(2)]
    acc_ref[...] = jnp.zeros(acc_ref.shape, F32)

    def scores(h, j, first_sub):
        off = pl.multiple_of(j * blk, blk)
        k = jnp.concatenate([kn_ref[pl.ds(off, blk), h * LANE:(h + 1) * LANE],
                             kp_ref[pl.ds(off, blk), h * LANE:(h + 1) * LANE]], axis=1)
        return jnp.dot(k, qts[h][:, first_sub * blk:], preferred_element_type=F32)

    ones_rows = jnp.ones((SUM_ROWS, blk), BF16)

    def consume(h, s, slot, vblk, m, cmax):
        m_new = jnp.maximum(m, cmax)
        alpha = jnp.exp2(m - m_new)
        pt = jnp.exp2(st_ref[h, s, slot] - m_new)
        vt = jnp.concatenate([vt_ref[vblk, h * V_DIM:(h + 1) * V_DIM, :], ones_rows], axis=0)
        acc_ref[h, s] = alpha * acc_ref[h, s] + jnp.dot(vt, pt.astype(BF16), preferred_element_type=F32)
        return m_new

    def colmax(st):
        return jnp.max(st, axis=0, keepdims=True)

    def step(t, carry, parity, first_sub=0, mask=None):
        carry = list(carry)
        for h in range(2):
            st = scores(h, t, first_sub)
            for s in range(max(first_sub - 1, 0), n_sub):
                m, cmax = carry[n_sub * h + s]
                m = consume(h, s, 1 - parity, t - 1, m, cmax)
                if s >= first_sub:
                    st_s = st[:, (s - first_sub) * blk:(s - first_sub + 1) * blk]
                    if s == first_sub and mask is not None:
                        st_s = jnp.where(mask, st_s, -jnp.inf)
                    st_ref[h, s, parity] = st_s
                    cmax = colmax(st_s)
                carry[n_sub * h + s] = (m, cmax)
        return tuple(carry)

    key = lax.broadcasted_iota(jnp.int32, (blk, blk), 0)
    qry = lax.broadcasted_iota(jnp.int32, (blk, blk), 1)
    causal = key <= qry
    m0 = jnp.full((1, blk), -jnp.inf, F32)

    first_mask = jnp.logical_or(causal, qi > 0)
    carry = []
    for h in range(2):
        st = scores(h, 0, 0)
        for s in range(n_sub):
            st_s = st[:, s * blk:(s + 1) * blk]
            if s == 0:
                st_s = jnp.where(first_mask, st_s, -jnp.inf)
            st_ref[h, s, 0] = st_s
            carry.append((m0, colmax(st_s)))

    def pair(u, carry):
        return step(2 * u + 2, step(2 * u + 1, carry, 1), 0)

    carry = lax.fori_loop(0, jnp.maximum(base // 2 - 1, 0), pair, tuple(carry))
    carry = lax.cond(qi >= 1,
                     lambda c: step(base, step(base - 1, c, 1), 0, 0, causal),
                     lambda c: c, carry)

    for k in range(1, n_sub):
        carry = step(base + k, carry, k % 2, k, causal)

    last = n_sub - 1
    for h in range(2):
        consume(h, last, last % 2, base + last, *carry[n_sub * h + last])
        for s in range(n_sub):
            out = acc_ref[h, s, :V_DIM, :] / acc_ref[h, s, V_DIM:V_DIM + 1, :]
            o_ref[s * blk:(s + 1) * blk, h * LANE:(h + 1) * LANE] = out.T.astype(BF16)


def _attention(qn, qp, kn, kpe, vt, batch, seq):
    tokens = qn.shape[1]
    blk, n_sub = ATTN_TILE, ATTN_SUB_TILES
    rows = n_sub * blk
    nk = seq // blk
    nq = seq // rows
    kern = functools.partial(_attn_kernel, blk=blk, n_sub=n_sub)
    return pl.pallas_call(
        kern,
        name="attention",
        grid=(batch, N_HEADS // 2, nq),
        in_specs=[
            pl.BlockSpec((2 * LANE, rows), lambda b, p, i: (p, b * nq + i)),
            pl.BlockSpec((LANE, rows), lambda b, p, i: (p, b * nq + i)),
            pl.BlockSpec((seq, 2 * LANE), lambda b, p, i: (b, p)),
            pl.BlockSpec((seq, 2 * LANE), lambda b, p, i: (b, 0)),
            pl.BlockSpec((nk, 2 * V_DIM, blk), lambda b, p, i: (b, p, 0)),
        ],
        out_specs=pl.BlockSpec((rows, 2 * LANE), lambda b, p, i: (b * nq + i, p)),
        out_shape=jax.ShapeDtypeStruct((tokens, N_HEADS * V_DIM), BF16),
        scratch_shapes=[pltpu.VMEM((2, n_sub, 2, blk, blk), F32),
                        pltpu.VMEM((2, n_sub, V_DIM + SUM_ROWS, blk), F32)],
        compiler_params=_params(("arbitrary", "arbitrary", "arbitrary")),
    )(qn, qp, kn, kpe, vt)


def _out_proj_kernel(x_ref, a_ref, w_ref, o_ref):
    o_ref[...] = x_ref[...] + jnp.dot(a_ref[...], w_ref[...], preferred_element_type=F32)


def _out_proj(x, attn, w_o):
    tokens, d = x.shape
    tm = ROW_TILE
    return pl.pallas_call(
        _out_proj_kernel,
        name="out_proj",
        grid=(tokens // tm,),
        in_specs=[
            pl.BlockSpec((tm, d), lambda i: (i, 0)),
            pl.BlockSpec((tm, attn.shape[1]), lambda i: (i, 0)),
            pl.BlockSpec(w_o.shape, lambda i: (0, 0)),
        ],
        out_specs=pl.BlockSpec((tm, d), lambda i: (i, 0)),
        out_shape=jax.ShapeDtypeStruct((tokens, d), F32),
        compiler_params=_params(("arbitrary",)),
    )(x, attn, w_o)


def _swap_halves(w):
    half = w.shape[-1] // 2
    return jnp.concatenate([w[..., half:], w[..., :half]], axis=-1)


def kernel(x, positions, pool_norm, pool_w, pool_scale, kv_in_norm, w_kv_a, kv_latent_norm, w_kv_b, attn_norm, w_q_a, q_latent_norm, w_q_b, w_o, ffn_norm, w_gate, w_up, w_down, final_norm):
    batch, seq, d = x.shape
    depth = ffn_norm.shape[0]
    n_pool = pool_norm.shape[0]
    assert seq % ROW_TILE == 0 and seq % (ATTN_SUB_TILES * ATTN_TILE) == 0 and ATTN_SUB_TILES % 2 == 0
    assert w_gate.shape[2] % FF_TILE == 0 and seq % FFN_ROW_TILE == 0

    half = QK_ROPE_DIM // 2
    inv_freq = (ROPE_BASE ** (-jnp.arange(half, dtype=F32) / half)).reshape(half, 1)
    pos = positions.astype(F32).reshape(1, batch * seq)
    row = lambda v: v.reshape(1, -1)
    xs = x.reshape(batch * seq, d)

    wg, wu, wd = w_gate, w_up.astype(BF16), w_down
    zpad = jnp.zeros((d, LANE - QK_ROPE_DIM), F32)
    w_pe = w_kv_a[:, KV_LORA_RANK:]
    wa_ext = jnp.concatenate([w_kv_a[:, :KV_LORA_RANK], w_pe, zpad, _swap_halves(w_pe), zpad], axis=1).astype(BF16)
    wkb = w_kv_b.reshape(KV_LORA_RANK, N_HEADS, QK_NOPE_DIM + V_DIM)
    w_kn = wkb[:, :, :QK_NOPE_DIM].reshape(KV_LORA_RANK, -1).astype(BF16)
    w_vt = wkb[:, :, QK_NOPE_DIM:].reshape(KV_LORA_RANK, -1).T.astype(BF16)

    kn = vt = kpe = None
    for l in range(depth):
        if l == n_pool:
            kn, vt, kpe = _kv_proj(xs, row(kv_in_norm), wa_ext, row(kv_latent_norm), w_kn, w_vt, pos, inv_freq)
        if l < n_pool:
            xs = _pool_mixer(xs, seq, row(pool_norm[l]), pool_w[l].astype(BF16), row(pool_scale[l]))
        else:
            b = l - n_pool
            rank = w_q_b.shape[1]
            wqb = w_q_b[b].reshape(rank, N_HEADS, QK_DIM)
            wqb_ext = jnp.concatenate([
                wqb[:, :, :QK_NOPE_DIM].reshape(rank, -1),
                wqb[:, :, QK_NOPE_DIM:].reshape(rank, -1)], axis=1).T.astype(BF16)
            qn, qp = _q_proj(xs, row(attn_norm[b]), w_q_a[b].astype(BF16), row(q_latent_norm[b]), wqb_ext, pos, inv_freq)
            attn = _attention(qn, qp, kn, kpe, vt, batch, seq)
            xs = _out_proj(xs, attn, w_o[b].astype(BF16))
        last = l == depth - 1
        xs = _ffn(xs, row(ffn_norm[l]), wg, wu, wd, row(final_norm), layer=l, final_norm=last)
    return xs.reshape(batch, seq, d)
```

```python
import functools
import math

import jax
import jax.numpy as jnp
from jax import lax
from jax.experimental import pallas as pl
from jax.experimental.pallas import tpu as pltpu

N_HEADS = 16
QK_NOPE_DIM = 128
QK_ROPE_DIM = 64
QK_DIM = QK_NOPE_DIM + QK_ROPE_DIM
V_DIM = 128
KV_LORA_RANK = 512
POOL_WINDOWS = (2, 4, 8, 16)
ROPE_BASE = 10000.0
NORM_EPS = 1e-6

LANE = 128
SUBLANE = 8
assert POOL_WINDOWS == tuple(2 ** (k + 1) for k in range(len(POOL_WINDOWS)))
HALO = SUBLANE * len(POOL_WINDOWS)
SUM_ROWS = 16
Q_PRESCALE = (1.0 / math.sqrt(QK_DIM)) * math.log2(math.e)

ROW_TILE = 512
FF_TILE = 512
FFN_ROW_TILE = 1024
FFN_SUB_ROWS = 512
ATTN_TILE = 512
ATTN_SUB_TILES = 2
VMEM_LIMIT = 56 * 1024 * 1024
FFN_VMEM_LIMIT = 60 * 1024 * 1024

F32 = jnp.float32
BF16 = jnp.bfloat16


def _rms(x, g):
    ms = jnp.mean(x * x, axis=-1, keepdims=True)
    return x * lax.rsqrt(ms + NORM_EPS) * g


def _rope_cos_sin(pos_ref, invf_ref):
    ang = invf_ref[...] * pos_ref[...]
    return jnp.cos(ang), jnp.sin(ang)


def _params(semantics, flags=None):
    return pltpu.CompilerParams(dimension_semantics=semantics, vmem_limit_bytes=VMEM_LIMIT, flags=flags)


def _pool_kernel(x_ref, halo_ref, g_ref, w_ref, sc_ref, o_ref, hh_ref, lv_ref, *, tiles_per_seq, tm, pg):
    t = pl.program_id(0) % tiles_per_seq
    g = g_ref[...]
    rows_end = HALO + tm
    hh_ref[HALO:, :] = _rms(x_ref[...], g)
    hh_ref[:HALO, :] = jnp.where(t == 0, 0.0, _rms(halo_ref[...], g))
    for k in range(1, len(POOL_WINDOWS) + 1):
        start, shift, cols = SUBLANE * k, 2 ** (k - 1), slice((k - 1) * pg, None)
        src = hh_ref if k == 1 else lv_ref.at[k % 2]
        lv_ref[(k - 1) % 2, start:rows_end, cols] = (src[start:rows_end, cols]
                                                     + src[start - shift:rows_end - shift, cols])
    pos = t * tm + lax.broadcasted_iota(jnp.int32, (tm, 1), 0)
    for gi, w in enumerate(POOL_WINDOWS):
        cols = slice(gi * pg, (gi + 1) * pg)
        count = jnp.minimum(pos + 1, w).astype(F32)
        diff = lv_ref[gi % 2, HALO:, cols] / count - hh_ref[HALO:, cols]
        y = jnp.dot(diff.astype(BF16), w_ref[gi], preferred_element_type=F32)
        o_ref[:, cols] = x_ref[:, cols] + y * sc_ref[:, cols]


def _pool_mixer(x, seq, norm, w, scale):
    tokens, d = x.shape
    tm = ROW_TILE
    pg = d // len(POOL_WINDOWS)
    kern = functools.partial(_pool_kernel, tiles_per_seq=seq // tm, tm=tm, pg=pg)
    return pl.pallas_call(
        kern,
        name="pool_mixer",
        grid=(tokens // tm,),
        in_specs=[
            pl.BlockSpec((tm, d), lambda i: (i, 0)),
            pl.BlockSpec((HALO, d), lambda i: (jnp.maximum(i * (tm // HALO) - 1, 0), 0)),
            pl.BlockSpec((1, d), lambda i: (0, 0)),
            pl.BlockSpec(w.shape, lambda i: (0, 0, 0)),
            pl.BlockSpec((1, d), lambda i: (0, 0)),
        ],
        out_specs=pl.BlockSpec((tm, d), lambda i: (i, 0)),
        out_shape=jax.ShapeDtypeStruct((tokens, d), F32),
        scratch_shapes=[pltpu.VMEM((HALO + tm, d), F32), pltpu.VMEM((2, HALO + tm, d), F32)],
        compiler_params=_params(("arbitrary",)),
    )(x, x, norm, w, scale)


def _ffn_kernel(x_ref, g_ref, wg_ref, wu_ref, wd_ref, fg_ref, o_ref, inv_ref, *, final_norm):
    j = pl.program_id(1)

    halves = [slice(r * FFN_SUB_ROWS, (r + 1) * FFN_SUB_ROWS) for r in range(x_ref.shape[0] // FFN_SUB_ROWS)]

    def chunk(first):
        for rows in halves:
            x = x_ref[rows, :]
            if first:
                inv = lax.rsqrt(jnp.mean(x * x, axis=-1, keepdims=True) + NORM_EPS)
                inv_ref[rows, :] = inv
            else:
                inv = inv_ref[rows, :]
            hn = (x * inv * g_ref[...]).astype(BF16)
            gate = jnp.dot(hn, wg_ref[...].astype(BF16), preferred_element_type=F32)
            up = jnp.dot(hn, wu_ref[...].astype(BF16), preferred_element_type=F32)
            act = gate / (1.0 + jnp.exp(-gate)) * up
            down = jnp.dot(act.astype(BF16), wd_ref[...].astype(BF16), preferred_element_type=F32)
            o_ref[rows, :] = (x_ref[rows, :] if first else o_ref[rows, :]) + down

    pl.when(j == 0)(functools.partial(chunk, True))
    pl.when(j > 0)(functools.partial(chunk, False))

    if final_norm:
        @pl.when(j == pl.num_programs(1) - 1)
        def _():
            for rows in halves:
                o_ref[rows, :] = _rms(o_ref[rows, :], fg_ref[...])


def _ffn(x, norm, w_gate, w_up, w_down, final_g, *, layer, final_norm):
    tokens, d = x.shape
    dff = w_gate.shape[2]
    tm, tf = FFN_ROW_TILE, FF_TILE
    kern = functools.partial(_ffn_kernel, final_norm=final_norm)
    return pl.pallas_call(
        kern,
        name="ffn_final" if final_norm else "ffn",
        grid=(tokens // tm, dff // tf),
        in_specs=[
            pl.BlockSpec((tm, d), lambda i, j: (i, 0)),
            pl.BlockSpec((1, d), lambda i, j: (0, 0)),
            pl.BlockSpec((None, d, tf), lambda i, j: (layer, 0, j)),
            pl.BlockSpec((None, d, tf), lambda i, j: (layer, 0, j)),
            pl.BlockSpec((None, tf, d), lambda i, j: (layer, j, 0)),
            pl.BlockSpec((1, d), lambda i, j: (0, 0)),
        ],
        out_specs=pl.BlockSpec((tm, d), lambda i, j: (i, 0)),
        out_shape=jax.ShapeDtypeStruct((tokens, d), F32),
        scratch_shapes=[pltpu.VMEM((tm, 1), F32)],
        compiler_params=pltpu.CompilerParams(dimension_semantics=("arbitrary", "arbitrary"),
                                             vmem_limit_bytes=FFN_VMEM_LIMIT),
    )(x, norm, w_gate, w_up, w_down, final_g)


def _kv_kernel(x_ref, g_ref, wa_ref, gl_ref, wkn_ref, wvt_ref, pos_ref, invf_ref, kn_ref, vt_ref, kpe_ref):
    h = _rms(x_ref[...], g_ref[...]).astype(BF16)
    kv = jnp.dot(h, wa_ref[...], preferred_element_type=F32)
    c_kv = _rms(kv[:, :KV_LORA_RANK], gl_ref[...]).astype(BF16)
    pe = kv[:, KV_LORA_RANK:KV_LORA_RANK + LANE]
    pe_swapped = kv[:, KV_LORA_RANK + LANE:]
    cos_t, sin_t = _rope_cos_sin(pos_ref, invf_ref)
    half = cos_t.shape[0]
    zeros = jnp.zeros((LANE - half, cos_t.shape[1]), F32)
    cos = jnp.concatenate([cos_t, zeros], axis=0).T
    sin = jnp.concatenate([sin_t, zeros], axis=0).T
    rot = (pe * cos - pe_swapped * sin) + pltpu.roll(pe_swapped * cos + pe * sin, half, axis=1)
    kpe_ref[:, :LANE] = rot.astype(BF16)
    kpe_ref[:, LANE:] = pltpu.roll(rot, LANE // 2, axis=1).astype(BF16)
    kn_ref[...] = jnp.dot(c_kv, wkn_ref[...], preferred_element_type=F32).astype(BF16)
    vt = lax.dot_general(wvt_ref[...], c_kv, (((1,), (1,)), ((), ())), preferred_element_type=F32)
    vt_ref[0] = vt.astype(BF16)


def _kv_proj(x, norm, wa_ext, latent_norm, w_kn, w_vt, pos, inv_freq):
    tokens, d = x.shape
    tm = ATTN_TILE
    n_kn, n_v = w_kn.shape[1], w_vt.shape[0]
    return pl.pallas_call(
        _kv_kernel,
        name="kv_proj",
        grid=(tokens // tm,),
        in_specs=[
            pl.BlockSpec((tm, d), lambda i: (i, 0)),
            pl.BlockSpec((1, d), lambda i: (0, 0)),
            pl.BlockSpec(wa_ext.shape, lambda i: (0, 0)),
            pl.BlockSpec((1, KV_LORA_RANK), lambda i: (0, 0)),
            pl.BlockSpec(w_kn.shape, lambda i: (0, 0)),
            pl.BlockSpec(w_vt.shape, lambda i: (0, 0)),
            pl.BlockSpec((1, tm), lambda i: (0, i)),
            pl.BlockSpec(inv_freq.shape, lambda i: (0, 0)),
        ],
        out_specs=[
            pl.BlockSpec((tm, n_kn), lambda i: (i, 0)),
            pl.BlockSpec((1, n_v, tm), lambda i: (i, 0, 0)),
            pl.BlockSpec((tm, 2 * LANE), lambda i: (i, 0)),
        ],
        out_shape=[
            jax.ShapeDtypeStruct((tokens, n_kn), BF16),
            jax.ShapeDtypeStruct((tokens // tm, n_v, tm), BF16),
            jax.ShapeDtypeStruct((tokens, 2 * LANE), BF16),
        ],
        compiler_params=_params(("arbitrary",)),
    )(x, norm, wa_ext, latent_norm, w_kn, w_vt, pos, inv_freq)


def _q_kernel(x_ref, g_ref, wa_ref, gl_ref, wb_ref, pos_ref, invf_ref, qn_ref, qp_ref, *, n_nope, n_pe):
    h = _rms(x_ref[...], g_ref[...]).astype(BF16)
    cq = jnp.dot(h, wa_ref[...], preferred_element_type=F32)
    cqn = _rms(cq, gl_ref[...]).astype(BF16)
    qt = lax.dot_general(wb_ref[...], cqn, (((1,), (1,)), ((), ())), preferred_element_type=F32)
    qn_ref[...] = (qt[:n_nope, :] * Q_PRESCALE).astype(BF16)
    cos_t, sin_t = _rope_cos_sin(pos_ref, invf_ref)
    cos_t, sin_t = cos_t * Q_PRESCALE, sin_t * Q_PRESCALE
    half = cos_t.shape[0]
    reps = LANE // (2 * half)
    cos = jnp.concatenate([cos_t, cos_t] * reps, axis=0)
    sin = jnp.concatenate([-sin_t, sin_t] * reps, axis=0)
    for p in range(n_pe // LANE):
        pe = qt[n_nope + p * LANE:n_nope + (p + 1) * LANE, :]
        sw = jnp.concatenate([pe[(2 * r + 1 - c) * half:(2 * r + 2 - c) * half, :]
                              for r in range(reps) for c in range(2)], axis=0)
        qp_ref[p * LANE:(p + 1) * LANE, :] = (pe * cos + sw * sin).astype(BF16)


def _q_proj(x, norm, wa, latent_norm, wb_ext, pos, inv_freq):
    tokens, d = x.shape
    tm = ROW_TILE
    n_nope = N_HEADS * QK_NOPE_DIM
    n_pe = N_HEADS * QK_ROPE_DIM
    kern = functools.partial(_q_kernel, n_nope=n_nope, n_pe=n_pe)
    return pl.pallas_call(
        kern,
        name="q_proj",
        grid=(tokens // tm,),
        in_specs=[
            pl.BlockSpec((tm, d), lambda i: (i, 0)),
            pl.BlockSpec((1, d), lambda i: (0, 0)),
            pl.BlockSpec(wa.shape, lambda i: (0, 0)),
            pl.BlockSpec((1, wa.shape[1]), lambda i: (0, 0)),
            pl.BlockSpec(wb_ext.shape, lambda i: (0, 0)),
            pl.BlockSpec((1, tm), lambda i: (0, i)),
            pl.BlockSpec(inv_freq.shape, lambda i: (0, 0)),
        ],
        out_specs=[
            pl.BlockSpec((n_nope, tm), lambda i: (0, i)),
            pl.BlockSpec((n_pe, tm), lambda i: (0, i)),
        ],
        out_shape=[
            jax.ShapeDtypeStruct((n_nope, tokens), BF16),
            jax.ShapeDtypeStruct((n_pe, tokens), BF16),
        ],
        compiler_params=_params(("arbitrary",)),
    )(x, norm, wa, latent_norm, wb_ext, pos, inv_freq)


def _attn_kernel(qn_ref, qp_ref, kn_ref, kp_ref, vt_ref, o_ref, st_ref, acc_ref, *, blk, n_sub):
    qi = pl.program_id(2)
    base = n_sub * qi
    qp = qp_ref[...]
    qts = [jnp.concatenate([qn_ref[h * LANE:(h + 1) * LANE, :], qp], axis=0) for h in range(2)]
    acc_ref[...] = jnp.zeros(acc_ref.shape, F32)

    def scores(h, j, first_sub):
        off = pl.multiple_of(j * blk, blk)
        k = jnp.concatenate([kn_ref[pl.ds(off, blk), h * LANE:(h + 1) * LANE],
                             kp_ref[pl.ds(off, blk), h * LANE:(h + 1) * LANE]], axis=1)
        return jnp.dot(k, qts[h][:, first_sub * blk:], preferred_element_type=F32)

    ones_rows = jnp.ones((SUM_ROWS, blk), BF16)

    def consume(h, s, slot, vblk, m, cmax):
        m_new = jnp.maximum(m, cmax)
        alpha = jnp.exp2(m - m_new)
        pt = jnp.exp2(st_ref[h, s, slot] - m_new)
        vt = jnp.concatenate([vt_ref[vblk, h * V_DIM:(h + 1) * V_DIM, :], ones_rows], axis=0)
        acc_ref[h, s] = alpha * acc_ref[h, s] + jnp.dot(vt, pt.astype(BF16), preferred_element_type=F32)
        return m_new

    def colmax(st):
        return jnp.max(st, axis=0, keepdims=True)

    def step(t, carry, parity, first_sub=0, mask=None):
        carry = list(carry)
        for h in range(2):
            st = scores(h, t, first_sub)
            for s in range(max(first_sub - 1, 0), n_sub):
                m, cmax = carry[n_sub * h + s]
                m = consume(h, s, 1 - parity, t - 1, m, cmax)
                if s >= first_sub:
                    st_s = st[:, (s - first_sub) * blk:(s - first_sub + 1) * blk]
                    if s == first_sub and mask is not None:
                        st_s = jnp.where(mask, st_s, -jnp.inf)
                    st_ref[h, s, parity] = st_s
                    cmax = colmax(st_s)
                carry[n_sub * h + s] = (m, cmax)
        return tuple(carry)

    key = lax.broadcasted_iota(jnp.int32, (blk, blk), 0)
    qry = lax.broadcasted_iota(jnp.int32, (blk, blk), 1)
    causal = key <= qry
    m0 = jnp.full((1, blk), -jnp.inf, F32)

    first_mask = jnp.logical_or(causal, qi > 0)
    carry = []
    for h in range(2):
        st = scores(h, 0, 0)
        for s in range(n_sub):
            st_s = st[:, s * blk:(s + 1) * blk]
            if s == 0:
                st_s = jnp.where(first_mask, st_s, -jnp.inf)
            st_ref[h, s, 0] = st_s
            carry.append((m0, colmax(st_s)))

    def pair(u, carry):
        return step(2 * u + 2, step(2 * u + 1, carry, 1), 0)

    carry = lax.fori_loop(0, jnp.maximum(base // 2 - 1, 0), pair, tuple(carry))
    carry = lax.cond(qi >= 1,
                     lambda c: step(base, step(base - 1, c, 1), 0, 0, causal),
                     lambda c: c, carry)

    for k in range(1, n_sub):
        carry = step(base + k, carry, k % 2, k, causal)

    last = n_sub - 1
    for h in range(2):
        consume(h, last, last % 2, base + last, *carry[n_sub * h + last])
        for s in range(n_sub):
            out = acc_ref[h, s, :V_DIM, :] / acc_ref[h, s, V_DIM:V_DIM + 1, :]
            o_ref[s * blk:(s + 1) * blk, h * LANE:(h + 1) * LANE] = out.T.astype(BF16)


def _attention(qn, qp, kn, kpe, vt, batch, seq):
    tokens = qn.shape[1]
    blk, n_sub = ATTN_TILE, ATTN_SUB_TILES
    rows = n_sub * blk
    nk = seq // blk
    nq = seq // rows
    kern = functools.partial(_attn_kernel, blk=blk, n_sub=n_sub)
    return pl.pallas_call(
        kern,
        name="attention",
        grid=(batch, N_HEADS // 2, nq),
        in_specs=[
            pl.BlockSpec((2 * LANE, rows), lambda b, p, i: (p, b * nq + i)),
            pl.BlockSpec((LANE, rows), lambda b, p, i: (p, b * nq + i)),
            pl.BlockSpec((seq, 2 * LANE), lambda b, p, i: (b, p)),
            pl.BlockSpec((seq, 2 * LANE), lambda b, p, i: (b, 0)),
            pl.BlockSpec((nk, 2 * V_DIM, blk), lambda b, p, i: (b, p, 0)),
        ],
        out_specs=pl.BlockSpec((rows, 2 * LANE), lambda b, p, i: (b * nq + i, p)),
        out_shape=jax.ShapeDtypeStruct((tokens, N_HEADS * V_DIM), BF16),
        scratch_shapes=[pltpu.VMEM((2, n_sub, 2, blk, blk), F32),
                        pltpu.VMEM((2, n_sub, V_DIM + SUM_ROWS, blk), F32)],
        compiler_params=_params(("arbitrary", "arbitrary", "arbitrary")),
    )(qn, qp, kn, kpe, vt)


def _out_proj_kernel(x_ref, a_ref, w_ref, o_ref):
    o_ref[...] = x_ref[...] + jnp.dot(a_ref[...], w_ref[...], preferred_element_type=F32)


def _out_proj(x, attn, w_o):
    tokens, d = x.shape
    tm = ROW_TILE
    return pl.pallas_call(
        _out_proj_kernel,
        name="out_proj",
        grid=(tokens // tm,),
        in_specs=[
            pl.BlockSpec((tm, d), lambda i: (i, 0)),
            pl.BlockSpec((tm, attn.shape[1]), lambda i: (i, 0)),
            pl.BlockSpec(w_o.shape, lambda i: (0, 0)),
        ],
        out_specs=pl.BlockSpec((tm, d), lambda i: (i, 0)),
        out_shape=jax.ShapeDtypeStruct((tokens, d), F32),
        compiler_params=_params(("arbitrary",)),
    )(x, attn, w_o)


def _swap_halves(w):
    half = w.shape[-1] // 2
    return jnp.concatenate([w[..., half:], w[..., :half]], axis=-1)


def kernel(x, positions, pool_norm, pool_w, pool_scale, kv_in_norm, w_kv_a, kv_latent_norm, w_kv_b, attn_norm, w_q_a, q_latent_norm, w_q_b, w_o, ffn_norm, w_gate, w_up, w_down, final_norm):
    batch, seq, d = x.shape
    depth = ffn_norm.shape[0]
    n_pool = pool_norm.shape[0]
    assert seq % ROW_TILE == 0 and seq % (ATTN_SUB_TILES * ATTN_TILE) == 0 and ATTN_SUB_TILES % 2 == 0
    assert w_gate.shape[2] % FF_TILE == 0 and seq % FFN_ROW_TILE == 0

    half = QK_ROPE_DIM // 2
    inv_freq = (ROPE_BASE ** (-jnp.arange(half, dtype=F32) / half)).reshape(half, 1)
    pos = positions.astype(F32).reshape(1, batch * seq)
    row = lambda v: v.reshape(1, -1)
    xs = x.reshape(batch * seq, d)

    wg, wu, wd = w_gate, w_up, w_down
    zpad = jnp.zeros((d, LANE - QK_ROPE_DIM), F32)
    w_pe = w_kv_a[:, KV_LORA_RANK:]
    wa_ext = jnp.concatenate([w_kv_a[:, :KV_LORA_RANK], w_pe, zpad, _swap_halves(w_pe), zpad], axis=1).astype(BF16)
    wkb = w_kv_b.reshape(KV_LORA_RANK, N_HEADS, QK_NOPE_DIM + V_DIM)
    w_kn = wkb[:, :, :QK_NOPE_DIM].reshape(KV_LORA_RANK, -1).astype(BF16)
    w_vt = wkb[:, :, QK_NOPE_DIM:].reshape(KV_LORA_RANK, -1).T.astype(BF16)

    kn = vt = kpe = None
    for l in range(depth):
        if l == n_pool:
            kn, vt, kpe = _kv_proj(xs, row(kv_in_norm), wa_ext, row(kv_latent_norm), w_kn, w_vt, pos, inv_freq)
        if l < n_pool:
            xs = _pool_mixer(xs, seq, row(pool_norm[l]), pool_w[l].astype(BF16), row(pool_scale[l]))
        else:
            b = l - n_pool
            rank = w_q_b.shape[1]
            wqb = w_q_b[b].reshape(rank, N_HEADS, QK_DIM)
            wqb_ext = jnp.concatenate([
                wqb[:, :, :QK_NOPE_DIM].reshape(rank, -1),
                wqb[:, :, QK_NOPE_DIM:].reshape(rank, -1)], axis=1).T.astype(BF16)
            qn, qp = _q_proj(xs, row(attn_norm[b]), w_q_a[b].astype(BF16), row(q_latent_norm[b]), wqb_ext, pos, inv_freq)
            attn = _attention(qn, qp, kn, kpe, vt, batch, seq)
            xs = _out_proj(xs, attn, w_o[b].astype(BF16))
        last = l == depth - 1
        xs = _ffn(xs, row(ffn_norm[l]), wg, wu, wd, row(final_norm), layer=l, final_norm=last)
    return xs.reshape(batch, seq, d)
```

```python
import functools
import math

import jax
import jax.numpy as jnp
from jax import lax
from jax.experimental import pallas as pl
from jax.experimental.pallas import tpu as pltpu

N_HEADS = 16
QK_NOPE_DIM = 128
QK_ROPE_DIM = 64
QK_DIM = QK_NOPE_DIM + QK_ROPE_DIM
V_DIM = 128
KV_LORA_RANK = 512
POOL_WINDOWS = (2, 4, 8, 16)
ROPE_BASE = 10000.0
NORM_EPS = 1e-6

LANE = 128
SUBLANE = 8
assert POOL_WINDOWS == tuple(2 ** (k + 1) for k in range(len(POOL_WINDOWS)))
HALO = SUBLANE * len(POOL_WINDOWS)
SUM_ROWS = 16
Q_PRESCALE = (1.0 / math.sqrt(QK_DIM)) * math.log2(math.e)

ROW_TILE = 512
FF_TILE = 512
FFN_ROW_TILE = 1024
FFN_SUB_ROWS = 512
ATTN_TILE = 512
ATTN_SUB_TILES = 2
VMEM_LIMIT = 56 * 1024 * 1024
FFN_VMEM_LIMIT = 60 * 1024 * 1024

F32 = jnp.float32
BF16 = jnp.bfloat16


def _rms(x, g):
    ms = jnp.mean(x * x, axis=-1, keepdims=True)
    return x * lax.rsqrt(ms + NORM_EPS) * g


def _rope_cos_sin(pos_ref, invf_ref):
    ang = invf_ref[...] * pos_ref[...]
    return jnp.cos(ang), jnp.sin(ang)


def _params(semantics, flags=None):
    return pltpu.CompilerParams(dimension_semantics=semantics, vmem_limit_bytes=VMEM_LIMIT, flags=flags)


def _pool_kernel(x_ref, halo_ref, g_ref, w_ref, sc_ref, o_ref, hh_ref, lv_ref, *, tiles_per_seq, tm, pg):
    t = pl.program_id(0) % tiles_per_seq
    g = g_ref[...]
    rows_end = HALO + tm
    hh_ref[HALO:, :] = _rms(x_ref[...], g)
    hh_ref[:HALO, :] = jnp.where(t == 0, 0.0, _rms(halo_ref[...], g))
    for k in range(1, len(POOL_WINDOWS) + 1):
        start, shift, cols = SUBLANE * k, 2 ** (k - 1), slice((k - 1) * pg, None)
        src = hh_ref if k == 1 else lv_ref.at[k % 2]
        lv_ref[(k - 1) % 2, start:rows_end, cols] = (src[start:rows_end, cols]
                                                     + src[start - shift:rows_end - shift, cols])
    pos = t * tm + lax.broadcasted_iota(jnp.int32, (tm, 1), 0)
    for gi, w in enumerate(POOL_WINDOWS):
        cols = slice(gi * pg, (gi + 1) * pg)
        count = jnp.minimum(pos + 1, w).astype(F32)
        diff = lv_ref[gi % 2, HALO:, cols] / count - hh_ref[HALO:, cols]
        y = jnp.dot(diff.astype(BF16), w_ref[gi], preferred_element_type=F32)
        o_ref[:, cols] = x_ref[:, cols] + y * sc_ref[:, cols]


def _pool_mixer(x, seq, norm, w, scale):
    tokens, d = x.shape
    tm = ROW_TILE
    pg = d // len(POOL_WINDOWS)
    kern = functools.partial(_pool_kernel, tiles_per_seq=seq // tm, tm=tm, pg=pg)
    return pl.pallas_call(
        kern,
        name="pool_mixer",
        grid=(tokens // tm,),
        in_specs=[
            pl.BlockSpec((tm, d), lambda i: (i, 0)),
            pl.BlockSpec((HALO, d), lambda i: (jnp.maximum(i * (tm // HALO) - 1, 0), 0)),
            pl.BlockSpec((1, d), lambda i: (0, 0)),
            pl.BlockSpec(w.shape, lambda i: (0, 0, 0)),
            pl.BlockSpec((1, d), lambda i: (0, 0)),
        ],
        out_specs=pl.BlockSpec((tm, d), lambda i: (i, 0)),
        out_shape=jax.ShapeDtypeStruct((tokens, d), F32),
        scratch_shapes=[pltpu.VMEM((HALO + tm, d), F32), pltpu.VMEM((2, HALO + tm, d), F32)],
        compiler_params=_params(("arbitrary",)),
    )(x, x, norm, w, scale)


def _ffn_kernel(x_ref, g_ref, wg_ref, wu_ref, wd_ref, fg_ref, o_ref, hn_ref, *, final_norm):
    j = pl.program_id(1)

    halves = [slice(r * FFN_SUB_ROWS, (r + 1) * FFN_SUB_ROWS) for r in range(x_ref.shape[0] // FFN_SUB_ROWS)]

    def chunk(first):
        for rows in halves:
            if first:
                hn = _rms(x_ref[rows, :], g_ref[...]).astype(BF16)
                hn_ref[rows, :] = hn
            else:
                hn = hn_ref[rows, :]
            gate = jnp.dot(hn, wg_ref[...].astype(BF16), preferred_element_type=F32)
            up = jnp.dot(hn, wu_ref[...], preferred_element_type=F32)
            act = gate / (1.0 + jnp.exp(-gate)) * up
            down = jnp.dot(act.astype(BF16), wd_ref[...].astype(BF16), preferred_element_type=F32)
            o_ref[rows, :] = (x_ref[rows, :] if first else o_ref[rows, :]) + down

    pl.when(j == 0)(functools.partial(chunk, True))
    pl.when(j > 0)(functools.partial(chunk, False))

    if final_norm:
        @pl.when(j == pl.num_programs(1) - 1)
        def _():
            for rows in halves:
                o_ref[rows, :] = _rms(o_ref[rows, :], fg_ref[...])


def _ffn(x, norm, w_gate, w_up, w_down, final_g, *, layer, final_norm):
    tokens, d = x.shape
    dff = w_gate.shape[2]
    tm, tf = FFN_ROW_TILE, FF_TILE
    kern = functools.partial(_ffn_kernel, final_norm=final_norm)
    return pl.pallas_call(
        kern,
        name="ffn_final" if final_norm else "ffn",
        grid=(tokens // tm, dff // tf),
        in_specs=[
            pl.BlockSpec((tm, d), lambda i, j: (i, 0)),
            pl.BlockSpec((1, d), lambda i, j: (0, 0)),
            pl.BlockSpec((None, d, tf), lambda i, j: (layer, 0, j)),
            pl.BlockSpec((None, d, tf), lambda i, j: (layer, 0, j)),
            pl.BlockSpec((None, tf, d), lambda i, j: (layer, j, 0)),
            pl.BlockSpec((1, d), lambda i, j: (0, 0)),
        ],
        out_specs=pl.BlockSpec((tm, d), lambda i, j: (i, 0)),
        out_shape=jax.ShapeDtypeStruct((tokens, d), F32),
        scratch_shapes=[pltpu.VMEM((tm, d), BF16)],
        compiler_params=pltpu.CompilerParams(dimension_semantics=("arbitrary", "arbitrary"),
                                             vmem_limit_bytes=FFN_VMEM_LIMIT),
    )(x, norm, w_gate, w_up, w_down, final_g)


def _kv_math(h, cos_t, sin_t, wa_ref, gl_ref, wkn_ref, wvt_ref, kn_ref, vt_ref, kpe_ref):
    kv = jnp.dot(h, wa_ref[...], preferred_element_type=F32)
    c_kv = _rms(kv[:, :KV_LORA_RANK], gl_ref[...]).astype(BF16)
    pe = kv[:, KV_LORA_RANK:KV_LORA_RANK + LANE]
    pe_swapped = kv[:, KV_LORA_RANK + LANE:]
    half = cos_t.shape[0]
    zeros = jnp.zeros((LANE - half, cos_t.shape[1]), F32)
    cos = jnp.concatenate([cos_t, zeros], axis=0).T
    sin = jnp.concatenate([sin_t, zeros], axis=0).T
    rot = (pe * cos - pe_swapped * sin) + pltpu.roll(pe_swapped * cos + pe * sin, half, axis=1)
    kpe_ref[:, :LANE] = rot.astype(BF16)
    kpe_ref[:, LANE:] = pltpu.roll(rot, LANE // 2, axis=1).astype(BF16)
    kn_ref[...] = jnp.dot(c_kv, wkn_ref[...], preferred_element_type=F32).astype(BF16)
    vt = lax.dot_general(wvt_ref[...], c_kv, (((1,), (1,)), ((), ())), preferred_element_type=F32)
    vt_ref[0] = vt.astype(BF16)


def _q_math(h, cos_t, sin_t, wa_ref, gl_ref, wb_ref, qn_ref, qp_ref):
    n_nope, n_pe = qn_ref.shape[0], qp_ref.shape[0]
    cq = jnp.dot(h, wa_ref[...], preferred_element_type=F32)
    cqn = _rms(cq, gl_ref[...]).astype(BF16)
    qt = lax.dot_general(wb_ref[...], cqn, (((1,), (1,)), ((), ())), preferred_element_type=F32)
    qn_ref[...] = (qt[:n_nope, :] * Q_PRESCALE).astype(BF16)
    cos_t, sin_t = cos_t * Q_PRESCALE, sin_t * Q_PRESCALE
    half = cos_t.shape[0]
    reps = LANE // (2 * half)
    cos = jnp.concatenate([cos_t, cos_t] * reps, axis=0)
    sin = jnp.concatenate([-sin_t, sin_t] * reps, axis=0)
    for p in range(n_pe // LANE):
        pe = qt[n_nope + p * LANE:n_nope + (p + 1) * LANE, :]
        sw = jnp.concatenate([pe[(2 * r + 1 - c) * half:(2 * r + 2 - c) * half, :]
                              for r in range(reps) for c in range(2)], axis=0)
        qp_ref[p * LANE:(p + 1) * LANE, :] = (pe * cos + sw * sin).astype(BF16)


def _qkv_kernel(x_ref, gkv_ref, gq_ref, wa_ref, gl_ref, wkn_ref, wvt_ref, wqa_ref, gql_ref, wqb_ref,
                pos_ref, invf_ref, kn_ref, vt_ref, kpe_ref, qn_ref, qp_ref):
    x = x_ref[...]
    xn = x * lax.rsqrt(jnp.mean(x * x, axis=-1, keepdims=True) + NORM_EPS)
    cos_t, sin_t = _rope_cos_sin(pos_ref, invf_ref)
    _kv_math((xn * gkv_ref[...]).astype(BF16), cos_t, sin_t, wa_ref, gl_ref, wkn_ref, wvt_ref, kn_ref, vt_ref, kpe_ref)
    _q_math((xn * gq_ref[...]).astype(BF16), cos_t, sin_t, wqa_ref, gql_ref, wqb_ref, qn_ref, qp_ref)


def _q_kernel(x_ref, g_ref, wa_ref, gl_ref, wb_ref, pos_ref, invf_ref, qn_ref, qp_ref):
    cos_t, sin_t = _rope_cos_sin(pos_ref, invf_ref)
    _q_math(_rms(x_ref[...], g_ref[...]).astype(BF16), cos_t, sin_t, wa_ref, gl_ref, wb_ref, qn_ref, qp_ref)


def _resident(shape):
    return pl.BlockSpec(shape, lambda i: (0,) * len(shape), pipeline_mode=pl.Buffered(1))


def _q_specs(tokens, tm, wb_ext):
    n_nope = N_HEADS * QK_NOPE_DIM
    n_pe = wb_ext.shape[0] - n_nope
    specs = [pl.BlockSpec((n_nope, tm), lambda i: (0, i)), pl.BlockSpec((n_pe, tm), lambda i: (0, i))]
    shapes = [jax.ShapeDtypeStruct((n_nope, tokens), BF16), jax.ShapeDtypeStruct((n_pe, tokens), BF16)]
    return specs, shapes


def _qkv_proj(x, kv_norm, q_norm, wa_ext, kv_latent_norm, w_kn, w_vt, wqa, q_latent_norm, wqb_ext, pos, inv_freq):
    tokens, d = x.shape
    tm = ATTN_TILE
    n_kn, n_v = w_kn.shape[1], w_vt.shape[0]
    q_specs, q_shapes = _q_specs(tokens, tm, wqb_ext)
    return pl.pallas_call(
        _qkv_kernel,
        name="qkv_proj",
        grid=(tokens // tm,),
        in_specs=[
            pl.BlockSpec((tm, d), lambda i: (i, 0)),
            _resident((1, d)), _resident((1, d)),
            _resident(wa_ext.shape), _resident((1, KV_LORA_RANK)), _resident(w_kn.shape), _resident(w_vt.shape),
            _resident(wqa.shape), _resident((1, wqa.shape[1])), _resident(wqb_ext.shape),
            pl.BlockSpec((1, tm), lambda i: (0, i)),
            _resident(inv_freq.shape),
        ],
        out_specs=[
            pl.BlockSpec((tm, n_kn), lambda i: (i, 0)),
            pl.BlockSpec((1, n_v, tm), lambda i: (i, 0, 0)),
            pl.BlockSpec((tm, 2 * LANE), lambda i: (i, 0)),
        ] + q_specs,
        out_shape=[
            jax.ShapeDtypeStruct((tokens, n_kn), BF16),
            jax.ShapeDtypeStruct((tokens // tm, n_v, tm), BF16),
            jax.ShapeDtypeStruct((tokens, 2 * LANE), BF16),
        ] + q_shapes,
        compiler_params=_params(("arbitrary",)),
    )(x, kv_norm, q_norm, wa_ext, kv_latent_norm, w_kn, w_vt, wqa, q_latent_norm, wqb_ext, pos, inv_freq)


def _q_proj(x, norm, wa, latent_norm, wb_ext, pos, inv_freq):
    tokens, d = x.shape
    tm = ROW_TILE
    q_specs, q_shapes = _q_specs(tokens, tm, wb_ext)
    return pl.pallas_call(
        _q_kernel,
        name="q_proj",
        grid=(tokens // tm,),
        in_specs=[
            pl.BlockSpec((tm, d), lambda i: (i, 0)),
            _resident((1, d)),
            _resident(wa.shape), _resident((1, wa.shape[1])), _resident(wb_ext.shape),
            pl.BlockSpec((1, tm), lambda i: (0, i)),
            _resident(inv_freq.shape),
        ],
        out_specs=q_specs,
        out_shape=q_shapes,
        compiler_params=_params(("arbitrary",)),
    )(x, norm, wa, latent_norm, wb_ext, pos, inv_freq)


def _attn_kernel(qn_ref, qp_ref, kn_ref, kp_ref, vt_ref, o_ref, st_ref, acc_ref, *, blk, n_sub):
    qi = pl.program_id(2)
    base = n_sub * qi
    qp = qp_ref[...]
    qts = [jnp.concatenate([qn_ref[h * LANE:(h + 1) * LANE, :], qp], axis=0) for h in range(2)]
    acc_ref[...] = jnp.zeros(acc_ref.shape, F32)

    def scores(h, j, first_sub):
        off = pl.multiple_of(j * blk, blk)
        k = jnp.concatenate([kn_ref[pl.ds(off, blk), h * LANE:(h + 1) * LANE],
                             kp_ref[pl.ds(off, blk), h * LANE:(h + 1) * LANE]], axis=1)
        return jnp.dot(k, qts[h][:, first_sub * blk:], preferred_element_type=F32)

    ones_rows = jnp.ones((SUM_ROWS, blk), BF16)

    def consume(h, s, slot, vblk, m, cmax):
        m_new = jnp.maximum(m, cmax)
        alpha = jnp.exp2(m - m_new)
        pt = jnp.exp2(st_ref[h, s, slot] - m_new)
        vt = jnp.concatenate([vt_ref[vblk, h * V_DIM:(h + 1) * V_DIM, :], ones_rows], axis=0)
        acc_ref[h, s] = alpha * acc_ref[h, s] + jnp.dot(vt, pt.astype(BF16), preferred_element_type=F32)
        return m_new

    def colmax(st):
        return jnp.max(st, axis=0, keepdims=True)

    def step(t, carry, parity, first_sub=0, mask=None):
        carry = list(carry)
        for h in range(2):
            st = scores(h, t, first_sub)
            for s in range(max(first_sub - 1, 0), n_sub):
                m, cmax = carry[n_sub * h + s]
                m = consume(h, s, 1 - parity, t - 1, m, cmax)
                if s >= first_sub:
                    st_s = st[:, (s - first_sub) * blk:(s - first_sub + 1) * blk]
                    if s == first_sub and mask is not None:
                        st_s = jnp.where(mask, st_s, -jnp.inf)
                    st_ref[h, s, parity] = st_s
                    cmax = colmax(st_s)
                carry[n_sub * h + s] = (m, cmax)
        return tuple(carry)

    key = lax.broadcasted_iota(jnp.int32, (blk, blk), 0)
    qry = lax.broadcasted_iota(jnp.int32, (blk, blk), 1)
    causal = key <= qry
    m0 = jnp.full((1, blk), -jnp.inf, F32)

    first_mask = jnp.logical_or(causal, qi > 0)
    carry = []
    for h in range(2):
        st = scores(h, 0, 0)
        for s in range(n_sub):
            st_s = st[:, s * blk:(s + 1) * blk]
            if s == 0:
                st_s = jnp.where(first_mask, st_s, -jnp.inf)
            st_ref[h, s, 0] = st_s
            carry.append((m0, colmax(st_s)))

    def pair(u, carry):
        return step(2 * u + 2, step(2 * u + 1, carry, 1), 0)

    carry = lax.fori_loop(0, jnp.maximum(base // 2 - 1, 0), pair, tuple(carry))
    carry = lax.cond(qi >= 1,
                     lambda c: step(base, step(base - 1, c, 1), 0, 0, causal),
                     lambda c: c, carry)

    for k in range(1, n_sub):
        carry = step(base + k, carry, k % 2, k, causal)

    last = n_sub - 1
    for h in range(2):
        consume(h, last, last % 2, base + last, *carry[n_sub * h + last])
        for s in range(n_sub):
            out = acc_ref[h, s, :V_DIM, :] / acc_ref[h, s, V_DIM:V_DIM + 1, :]
            o_ref[s * blk:(s + 1) * blk, h * LANE:(h + 1) * LANE] = out.T.astype(BF16)


def _attention(qn, qp, kn, kpe, vt, batch, seq):
    tokens = qn.shape[1]
    blk, n_sub = ATTN_TILE, ATTN_SUB_TILES
    rows = n_sub * blk
    nk = seq // blk
    nq = seq // rows
    kern = functools.partial(_attn_kernel, blk=blk, n_sub=n_sub)
    return pl.pallas_call(
        kern,
        name="attention",
        grid=(batch, N_HEADS // 2, nq),
        in_specs=[
            pl.BlockSpec((2 * LANE, rows), lambda b, p, i: (p, b * nq + i)),
            pl.BlockSpec((LANE, rows), lambda b, p, i: (p, b * nq + i)),
            pl.BlockSpec((seq, 2 * LANE), lambda b, p, i: (b, p)),
            pl.BlockSpec((seq, 2 * LANE), lambda b, p, i: (b, 0)),
            pl.BlockSpec((nk, 2 * V_DIM, blk), lambda b, p, i: (b, p, 0)),
        ],
        out_specs=pl.BlockSpec((rows, 2 * LANE), lambda b, p, i: (b * nq + i, p)),
        out_shape=jax.ShapeDtypeStruct((tokens, N_HEADS * V_DIM), BF16),
        scratch_shapes=[pltpu.VMEM((2, n_sub, 2, blk, blk), F32),
                        pltpu.VMEM((2, n_sub, V_DIM + SUM_ROWS, blk), F32)],
        compiler_params=_params(("arbitrary", "arbitrary", "arbitrary")),
    )(qn, qp, kn, kpe, vt)


def _out_proj_kernel(x_ref, a_ref, w_ref, o_ref):
    o_ref[...] = x_ref[...] + jnp.dot(a_ref[...], w_ref[...], preferred_element_type=F32)


def _out_proj(x, attn, w_o):
    tokens, d = x.shape
    tm = ROW_TILE
    return pl.pallas_call(
        _out_proj_kernel,
        name="out_proj",
        grid=(tokens // tm,),
        in_specs=[
            pl.BlockSpec((tm, d), lambda i: (i, 0)),
            pl.BlockSpec((tm, attn.shape[1]), lambda i: (i, 0)),
            pl.BlockSpec(w_o.shape, lambda i: (0, 0)),
        ],
        out_specs=pl.BlockSpec((tm, d), lambda i: (i, 0)),
        out_shape=jax.ShapeDtypeStruct((tokens, d), F32),
        compiler_params=_params(("arbitrary",)),
    )(x, attn, w_o)


def _swap_halves(w):
    half = w.shape[-1] // 2
    return jnp.concatenate([w[..., half:], w[..., :half]], axis=-1)


def kernel(x, positions, pool_norm, pool_w, pool_scale, kv_in_norm, w_kv_a, kv_latent_norm, w_kv_b, attn_norm, w_q_a, q_latent_norm, w_q_b, w_o, ffn_norm, w_gate, w_up, w_down, final_norm):
    batch, seq, d = x.shape
    depth = ffn_norm.shape[0]
    n_pool = pool_norm.shape[0]
    assert seq % ROW_TILE == 0 and seq % (ATTN_SUB_TILES * ATTN_TILE) == 0 and ATTN_SUB_TILES % 2 == 0
    assert w_gate.shape[2] % FF_TILE == 0 and seq % FFN_ROW_TILE == 0

    half = QK_ROPE_DIM // 2
    inv_freq = (ROPE_BASE ** (-jnp.arange(half, dtype=F32) / half)).reshape(half, 1)
    pos = positions.astype(F32).reshape(1, batch * seq)
    row = lambda v: v.reshape(1, -1)
    xs = x.reshape(batch * seq, d)

    wg, wu, wd = w_gate, w_up.astype(BF16), w_down
    zpad = jnp.zeros((d, LANE - QK_ROPE_DIM), F32)
    w_pe = w_kv_a[:, KV_LORA_RANK:]
    wa_ext = jnp.concatenate([w_kv_a[:, :KV_LORA_RANK], w_pe, zpad, _swap_halves(w_pe), zpad], axis=1).astype(BF16)
    wkb = w_kv_b.reshape(KV_LORA_RANK, N_HEADS, QK_NOPE_DIM + V_DIM)
    w_kn = wkb[:, :, :QK_NOPE_DIM].reshape(KV_LORA_RANK, -1).astype(BF16)
    w_vt = wkb[:, :, QK_NOPE_DIM:].reshape(KV_LORA_RANK, -1).T.astype(BF16)

    kn = vt = kpe = None
    for l in range(depth):
        if l < n_pool:
            xs = _pool_mixer(xs, seq, row(pool_norm[l]), pool_w[l].astype(BF16), row(pool_scale[l]))
        else:
            b = l - n_pool
            rank = w_q_b.shape[1]
            wqb = w_q_b[b].reshape(rank, N_HEADS, QK_DIM)
            wqb_ext = jnp.concatenate([
                wqb[:, :, :QK_NOPE_DIM].reshape(rank, -1),
                wqb[:, :, QK_NOPE_DIM:].reshape(rank, -1)], axis=1).T.astype(BF16)
            wqa = w_q_a[b].astype(BF16)
            if b == 0:
                kn, vt, kpe, qn, qp = _qkv_proj(xs, row(kv_in_norm), row(attn_norm[b]), wa_ext, row(kv_latent_norm),
                                                w_kn, w_vt, wqa, row(q_latent_norm[b]), wqb_ext, pos, inv_freq)
            else:
                qn, qp = _q_proj(xs, row(attn_norm[b]), wqa, row(q_latent_norm[b]), wqb_ext, pos, inv_freq)
            attn = _attention(qn, qp, kn, kpe, vt, batch, seq)
            xs = _out_proj(xs, attn, w_o[b].astype(BF16))
        last = l == depth - 1
        xs = _ffn(xs, row(ffn_norm[l]), wg, wu, wd, row(final_norm), layer=l, final_norm=last)
    return xs.reshape(batch, seq, d)
```

```python
import functools
import math

import jax
import jax.numpy as jnp
from jax import lax
from jax.experimental import pallas as pl
from jax.experimental.pallas import tpu as pltpu

N_HEADS = 16
QK_NOPE_DIM = 128
QK_ROPE_DIM = 64
QK_DIM = QK_NOPE_DIM + QK_ROPE_DIM
V_DIM = 128
KV_LORA_RANK = 512
POOL_WINDOWS = (2, 4, 8, 16)
ROPE_BASE = 10000.0
NORM_EPS = 1e-6

LANE = 128
SUBLANE = 8
assert POOL_WINDOWS == tuple(2 ** (k + 1) for k in range(len(POOL_WINDOWS)))
HALO = SUBLANE * len(POOL_WINDOWS)
SUM_ROWS = 16
Q_PRESCALE = (1.0 / math.sqrt(QK_DIM)) * math.log2(math.e)

ROW_TILE = 512
FF_TILE = 512
FFN_ROW_TILE = 1024
FFN_SUB_ROWS = 512
ATTN_TILE = 512
ATTN_SUB_TILES = 2
VMEM_LIMIT = 56 * 1024 * 1024
FFN_VMEM_LIMIT = 60 * 1024 * 1024

F32 = jnp.float32
BF16 = jnp.bfloat16


def _rms(x, g):
    ms = jnp.mean(x * x, axis=-1, keepdims=True)
    return x * lax.rsqrt(ms + NORM_EPS) * g


def _rope_cos_sin(pos_ref, invf_ref):
    ang = invf_ref[...] * pos_ref[...]
    return jnp.cos(ang), jnp.sin(ang)


def _params(semantics, vmem_limit=VMEM_LIMIT):
    return pltpu.CompilerParams(dimension_semantics=semantics, vmem_limit_bytes=vmem_limit)


def _pool_kernel(x_ref, halo_ref, g_ref, w_ref, sc_ref, o_ref, hh_ref, lv_ref, *, tiles_per_seq, tm, pg):
    t = pl.program_id(0) % tiles_per_seq
    g = g_ref[...]
    rows_end = HALO + tm
    hh_ref[HALO:, :] = _rms(x_ref[...], g)
    hh_ref[:HALO, :] = jnp.where(t == 0, 0.0, _rms(halo_ref[...], g))
    for k in range(1, len(POOL_WINDOWS) + 1):
        start, shift, cols = SUBLANE * k, 2 ** (k - 1), slice((k - 1) * pg, None)
        src = hh_ref if k == 1 else lv_ref.at[k % 2]
        lv_ref[(k - 1) % 2, start:rows_end, cols] = (src[start:rows_end, cols]
                                                     + src[start - shift:rows_end - shift, cols])
    pos = t * tm + lax.broadcasted_iota(jnp.int32, (tm, 1), 0)
    for gi, w in enumerate(POOL_WINDOWS):
        cols = slice(gi * pg, (gi + 1) * pg)
        count = jnp.minimum(pos + 1, w).astype(F32)
        diff = lv_ref[gi % 2, HALO:, cols] / count - hh_ref[HALO:, cols]
        y = jnp.dot(diff.astype(BF16), w_ref[gi], preferred_element_type=F32)
        o_ref[:, cols] = x_ref[:, cols] + y * sc_ref[:, cols]


def _pool_mixer(x, seq, norm, w, scale):
    tokens, d = x.shape
    tm = ROW_TILE
    pg = d // len(POOL_WINDOWS)
    kern = functools.partial(_pool_kernel, tiles_per_seq=seq // tm, tm=tm, pg=pg)
    return pl.pallas_call(
        kern,
        name="pool_mixer",
        grid=(tokens // tm,),
        in_specs=[
            pl.BlockSpec((tm, d), lambda i: (i, 0)),
            pl.BlockSpec((HALO, d), lambda i: (jnp.maximum(i * (tm // HALO) - 1, 0), 0)),
            pl.BlockSpec((1, d), lambda i: (0, 0)),
            pl.BlockSpec(w.shape, lambda i: (0, 0, 0)),
            pl.BlockSpec((1, d), lambda i: (0, 0)),
        ],
        out_specs=pl.BlockSpec((tm, d), lambda i: (i, 0)),
        out_shape=jax.ShapeDtypeStruct((tokens, d), F32),
        scratch_shapes=[pltpu.VMEM((HALO + tm, d), F32), pltpu.VMEM((2, HALO + tm, d), F32)],
        compiler_params=_params(("arbitrary",)),
    )(x, x, norm, w, scale)


def _ffn_kernel(x_ref, g_ref, wg_ref, wu_ref, wd_ref, fg_ref, o_ref, hn_ref, *, final_norm):
    j = pl.program_id(1)

    halves = [slice(r * FFN_SUB_ROWS, (r + 1) * FFN_SUB_ROWS) for r in range(x_ref.shape[0] // FFN_SUB_ROWS)]

    def chunk(first):
        for rows in halves:
            if first:
                hn = _rms(x_ref[rows, :], g_ref[...]).astype(BF16)
                hn_ref[rows, :] = hn
            else:
                hn = hn_ref[rows, :]
            gate = jnp.dot(hn, wg_ref[...].astype(BF16), preferred_element_type=F32)
            up = jnp.dot(hn, wu_ref[...], preferred_element_type=F32)
            act = gate / (1.0 + jnp.exp(-gate)) * up
            down = jnp.dot(act.astype(BF16), wd_ref[...].astype(BF16), preferred_element_type=F32)
            o_ref[rows, :] = (x_ref[rows, :] if first else o_ref[rows, :]) + down

    pl.when(j == 0)(functools.partial(chunk, True))
    pl.when(j > 0)(functools.partial(chunk, False))

    if final_norm:
        @pl.when(j == pl.num_programs(1) - 1)
        def _():
            for rows in halves:
                o_ref[rows, :] = _rms(o_ref[rows, :], fg_ref[...])


def _ffn(x, norm, w_gate, w_up, w_down, final_g, *, layer, final_norm):
    tokens, d = x.shape
    dff = w_gate.shape[2]
    tm, tf = FFN_ROW_TILE, FF_TILE
    kern = functools.partial(_ffn_kernel, final_norm=final_norm)
    return pl.pallas_call(
        kern,
        name="ffn_final" if final_norm else "ffn",
        grid=(tokens // tm, dff // tf),
        in_specs=[
            pl.BlockSpec((tm, d), lambda i, j: (i, 0)),
            pl.BlockSpec((1, d), lambda i, j: (0, 0)),
            pl.BlockSpec((None, d, tf), lambda i, j: (layer, 0, j)),
            pl.BlockSpec((None, d, tf), lambda i, j: (layer, 0, j)),
            pl.BlockSpec((None, tf, d), lambda i, j: (layer, j, 0)),
            pl.BlockSpec((1, d), lambda i, j: (0, 0)),
        ],
        out_specs=pl.BlockSpec((tm, d), lambda i, j: (i, 0)),
        out_shape=jax.ShapeDtypeStruct((tokens, d), F32),
        scratch_shapes=[pltpu.VMEM((tm, d), BF16)],
        compiler_params=_params(("arbitrary", "arbitrary"), FFN_VMEM_LIMIT),
    )(x, norm, w_gate, w_up, w_down, final_g)


def _kv_math(h, cos_t, sin_t, wa_ref, gl_ref, wkn_ref, wvt_ref, kn_ref, vt_ref, kpe_ref):
    kv = jnp.dot(h, wa_ref[...], preferred_element_type=F32)
    c_kv = _rms(kv[:, :KV_LORA_RANK], gl_ref[...]).astype(BF16)
    pe = kv[:, KV_LORA_RANK:KV_LORA_RANK + LANE]
    pe_swapped = kv[:, KV_LORA_RANK + LANE:]
    half = cos_t.shape[0]
    zeros = jnp.zeros((LANE - half, cos_t.shape[1]), F32)
    cos = jnp.concatenate([cos_t, zeros], axis=0).T
    sin = jnp.concatenate([sin_t, zeros], axis=0).T
    rot = (pe * cos - pe_swapped * sin) + pltpu.roll(pe_swapped * cos + pe * sin, half, axis=1)
    kpe_ref[:, :LANE] = rot.astype(BF16)
    kpe_ref[:, LANE:] = pltpu.roll(rot, LANE // 2, axis=1).astype(BF16)
    kn_ref[...] = jnp.dot(c_kv, wkn_ref[...], preferred_element_type=F32).astype(BF16)
    vt = lax.dot_general(wvt_ref[...], c_kv, (((1,), (1,)), ((), ())), preferred_element_type=F32)
    vt_ref[0] = vt.astype(BF16)


def _q_math(h, cos_t, sin_t, wa_ref, gl_ref, wb_ref, qn_ref, qp_ref):
    n_nope, n_pe = qn_ref.shape[0], qp_ref.shape[0]
    cq = jnp.dot(h, wa_ref[...], preferred_element_type=F32)
    cqn = _rms(cq, gl_ref[...]).astype(BF16)
    qt = lax.dot_general(wb_ref[...], cqn, (((1,), (1,)), ((), ())), preferred_element_type=F32)
    qn_ref[...] = (qt[:n_nope, :] * Q_PRESCALE).astype(BF16)
    cos_t, sin_t = cos_t * Q_PRESCALE, sin_t * Q_PRESCALE
    half = cos_t.shape[0]
    reps = LANE // (2 * half)
    cos = jnp.concatenate([cos_t, cos_t] * reps, axis=0)
    sin = jnp.concatenate([-sin_t, sin_t] * reps, axis=0)
    for p in range(n_pe // LANE):
        pe = qt[n_nope + p * LANE:n_nope + (p + 1) * LANE, :]
        sw = jnp.concatenate([pe[(2 * r + 1 - c) * half:(2 * r + 2 - c) * half, :]
                              for r in range(reps) for c in range(2)], axis=0)
        qp_ref[p * LANE:(p + 1) * LANE, :] = (pe * cos + sw * sin).astype(BF16)


def _qkv_kernel(x_ref, gkv_ref, gq_ref, wa_ref, gl_ref, wkn_ref, wvt_ref, wqa_ref, gql_ref, wqb_ref,
                pos_ref, invf_ref, kn_ref, vt_ref, kpe_ref, qn_ref, qp_ref):
    x = x_ref[...]
    xn = x * lax.rsqrt(jnp.mean(x * x, axis=-1, keepdims=True) + NORM_EPS)
    cos_t, sin_t = _rope_cos_sin(pos_ref, invf_ref)
    _kv_math((xn * gkv_ref[...]).astype(BF16), cos_t, sin_t, wa_ref, gl_ref, wkn_ref, wvt_ref, kn_ref, vt_ref, kpe_ref)
    _q_math((xn * gq_ref[...]).astype(BF16), cos_t, sin_t, wqa_ref, gql_ref, wqb_ref, qn_ref, qp_ref)


def _q_kernel(x_ref, g_ref, wa_ref, gl_ref, wb_ref, pos_ref, invf_ref, qn_ref, qp_ref):
    cos_t, sin_t = _rope_cos_sin(pos_ref, invf_ref)
    _q_math(_rms(x_ref[...], g_ref[...]).astype(BF16), cos_t, sin_t, wa_ref, gl_ref, wb_ref, qn_ref, qp_ref)


def _resident(shape):
    return pl.BlockSpec(shape, lambda i: (0,) * len(shape), pipeline_mode=pl.Buffered(1))


def _q_specs(tokens, tm, wb_ext):
    n_nope = N_HEADS * QK_NOPE_DIM
    n_pe = wb_ext.shape[0] - n_nope
    specs = [pl.BlockSpec((n_nope, tm), lambda i: (0, i)), pl.BlockSpec((n_pe, tm), lambda i: (0, i))]
    shapes = [jax.ShapeDtypeStruct((n_nope, tokens), BF16), jax.ShapeDtypeStruct((n_pe, tokens), BF16)]
    return specs, shapes


def _qkv_proj(x, kv_norm, q_norm, wa_ext, kv_latent_norm, w_kn, w_vt, wqa, q_latent_norm, wqb_ext, pos, inv_freq):
    tokens, d = x.shape
    tm = ATTN_TILE
    n_kn, n_v = w_kn.shape[1], w_vt.shape[0]
    q_specs, q_shapes = _q_specs(tokens, tm, wqb_ext)
    return pl.pallas_call(
        _qkv_kernel,
        name="qkv_proj",
        grid=(tokens // tm,),
        in_specs=[
            pl.BlockSpec((tm, d), lambda i: (i, 0)),
            _resident((1, d)), _resident((1, d)),
            _resident(wa_ext.shape), _resident((1, KV_LORA_RANK)), _resident(w_kn.shape), _resident(w_vt.shape),
            _resident(wqa.shape), _resident((1, wqa.shape[1])), _resident(wqb_ext.shape),
            pl.BlockSpec((1, tm), lambda i: (0, i)),
            _resident(inv_freq.shape),
        ],
        out_specs=[
            pl.BlockSpec((tm, n_kn), lambda i: (i, 0)),
            pl.BlockSpec((1, n_v, tm), lambda i: (i, 0, 0)),
            pl.BlockSpec((tm, 2 * LANE), lambda i: (i, 0)),
        ] + q_specs,
        out_shape=[
            jax.ShapeDtypeStruct((tokens, n_kn), BF16),
            jax.ShapeDtypeStruct((tokens // tm, n_v, tm), BF16),
            jax.ShapeDtypeStruct((tokens, 2 * LANE), BF16),
        ] + q_shapes,
        compiler_params=_params(("arbitrary",)),
    )(x, kv_norm, q_norm, wa_ext, kv_latent_norm, w_kn, w_vt, wqa, q_latent_norm, wqb_ext, pos, inv_freq)


def _q_proj(x, norm, wa, latent_norm, wb_ext, pos, inv_freq):
    tokens, d = x.shape
    tm = ROW_TILE
    q_specs, q_shapes = _q_specs(tokens, tm, wb_ext)
    return pl.pallas_call(
        _q_kernel,
        name="q_proj",
        grid=(tokens // tm,),
        in_specs=[
            pl.BlockSpec((tm, d), lambda i: (i, 0)),
            _resident((1, d)),
            _resident(wa.shape), _resident((1, wa.shape[1])), _resident(wb_ext.shape),
            pl.BlockSpec((1, tm), lambda i: (0, i)),
            _resident(inv_freq.shape),
        ],
        out_specs=q_specs,
        out_shape=q_shapes,
        compiler_params=_params(("arbitrary",)),
    )(x, norm, wa, latent_norm, wb_ext, pos, inv_freq)


def _attn_kernel(qn_ref, qp_ref, kn_ref, kp_ref, vt_ref, o_ref, st_ref, acc_ref, *, blk, n_sub):
    qi = pl.program_id(2)
    base = n_sub * qi
    qp = qp_ref[...]
    qts = [jnp.concatenate([qn_ref[h * LANE:(h + 1) * LANE, :], qp], axis=0) for h in range(2)]
    acc_ref[...] = jnp.zeros(acc_ref.shape, F32)

    def scores(h, j, first_sub):
        off = pl.multiple_of(j * blk, blk)
        k = jnp.concatenate([kn_ref[pl.ds(off, blk), h * LANE:(h + 1) * LANE],
                             kp_ref[pl.ds(off, blk), h * LANE:(h + 1) * LANE]], axis=1)
        return jnp.dot(k, qts[h][:, first_sub * blk:], preferred_element_type=F32)

    ones_rows = jnp.ones((SUM_ROWS, blk), BF16)

    def consume(h, s, slot, vblk, m, cmax):
        m_new = jnp.maximum(m, cmax)
        alpha = jnp.exp2(m - m_new)
        pt = jnp.exp2(st_ref[h, s, slot] - m_new)
        vt = jnp.concatenate([vt_ref[vblk, h * V_DIM:(h + 1) * V_DIM, :], ones_rows], axis=0)
        acc_ref[h, s] = alpha * acc_ref[h, s] + jnp.dot(vt, pt.astype(BF16), preferred_element_type=F32)
        return m_new

    def colmax(st):
        return jnp.max(st, axis=0, keepdims=True)

    def step(t, carry, parity, first_sub=0, mask=None):
        carry = list(carry)
        for h in range(2):
            st = scores(h, t, first_sub)
            for s in range(max(first_sub - 1, 0), n_sub):
                m, cmax = carry[n_sub * h + s]
                m = consume(h, s, 1 - parity, t - 1, m, cmax)
                if s >= first_sub:
                    st_s = st[:, (s - first_sub) * blk:(s - first_sub + 1) * blk]
                    if s == first_sub and mask is not None:
                        st_s = jnp.where(mask, st_s, -jnp.inf)
                    st_ref[h, s, parity] = st_s
                    cmax = colmax(st_s)
                carry[n_sub * h + s] = (m, cmax)
        return tuple(carry)

    key = lax.broadcasted_iota(jnp.int32, (blk, blk), 0)
    qry = lax.broadcasted_iota(jnp.int32, (blk, blk), 1)
    causal = key <= qry
    m0 = jnp.full((1, blk), -jnp.inf, F32)

    first_mask = jnp.logical_or(causal, qi > 0)
    carry = []
    for h in range(2):
        st = scores(h, 0, 0)
        for s in range(n_sub):
            st_s = st[:, s * blk:(s + 1) * blk]
            if s == 0:
                st_s = jnp.where(first_mask, st_s, -jnp.inf)
            st_ref[h, s, 0] = st_s
            carry.append((m0, colmax(st_s)))

    def pair(u, carry):
        return step(2 * u + 2, step(2 * u + 1, carry, 1), 0)

    carry = lax.fori_loop(0, jnp.maximum(base // 2 - 1, 0), pair, tuple(carry))
    carry = lax.cond(qi >= 1,
                     lambda c: step(base, step(base - 1, c, 1), 0, 0, causal),
                     lambda c: c, carry)

    for k in range(1, n_sub):
        carry = step(base + k, carry, k % 2, k, causal)

    last = n_sub - 1
    for h in range(2):
        consume(h, last, last % 2, base + last, *carry[n_sub * h + last])
        for s in range(n_sub):
            out = acc_ref[h, s, :V_DIM, :] / acc_ref[h, s, V_DIM:V_DIM + 1, :]
            o_ref[s * blk:(s + 1) * blk, h * LANE:(h + 1) * LANE] = out.T.astype(BF16)


def _attention(qn, qp, kn, kpe, vt, batch, seq):
    tokens = qn.shape[1]
    blk, n_sub = ATTN_TILE, ATTN_SUB_TILES
    rows = n_sub * blk
    nk = seq // blk
    nq = seq // rows
    kern = functools.partial(_attn_kernel, blk=blk, n_sub=n_sub)
    return pl.pallas_call(
        kern,
        name="attention",
        grid=(batch, N_HEADS // 2, nq),
        in_specs=[
            pl.BlockSpec((2 * LANE, rows), lambda b, p, i: (p, b * nq + i)),
            pl.BlockSpec((LANE, rows), lambda b, p, i: (p, b * nq + i)),
            pl.BlockSpec((seq, 2 * LANE), lambda b, p, i: (b, p)),
            pl.BlockSpec((seq, 2 * LANE), lambda b, p, i: (b, 0)),
            pl.BlockSpec((nk, 2 * V_DIM, blk), lambda b, p, i: (b, p, 0)),
        ],
        out_specs=pl.BlockSpec((rows, 2 * LANE), lambda b, p, i: (b * nq + i, p)),
        out_shape=jax.ShapeDtypeStruct((tokens, N_HEADS * V_DIM), BF16),
        scratch_shapes=[pltpu.VMEM((2, n_sub, 2, blk, blk), F32),
                        pltpu.VMEM((2, n_sub, V_DIM + SUM_ROWS, blk), F32)],
        compiler_params=_params(("arbitrary", "arbitrary", "arbitrary")),
    )(qn, qp, kn, kpe, vt)


def _out_proj_kernel(x_ref, a_ref, w_ref, o_ref):
    o_ref[...] = x_ref[...] + jnp.dot(a_ref[...], w_ref[...], preferred_element_type=F32)


def _out_proj(x, attn, w_o):
    tokens, d = x.shape
    tm = ROW_TILE
    return pl.pallas_call(
        _out_proj_kernel,
        name="out_proj",
        grid=(tokens // tm,),
        in_specs=[
            pl.BlockSpec((tm, d), lambda i: (i, 0)),
            pl.BlockSpec((tm, attn.shape[1]), lambda i: (i, 0)),
            pl.BlockSpec(w_o.shape, lambda i: (0, 0)),
        ],
        out_specs=pl.BlockSpec((tm, d), lambda i: (i, 0)),
        out_shape=jax.ShapeDtypeStruct((tokens, d), F32),
        compiler_params=_params(("arbitrary",)),
    )(x, attn, w_o)


def _swap_halves(w):
    half = w.shape[-1] // 2
    return jnp.concatenate([w[..., half:], w[..., :half]], axis=-1)


def kernel(x, positions, pool_norm, pool_w, pool_scale, kv_in_norm, w_kv_a, kv_latent_norm, w_kv_b, attn_norm, w_q_a, q_latent_norm, w_q_b, w_o, ffn_norm, w_gate, w_up, w_down, final_norm):
    batch, seq, d = x.shape
    depth = ffn_norm.shape[0]
    n_pool = pool_norm.shape[0]
    assert seq % ROW_TILE == 0 and seq % (ATTN_SUB_TILES * ATTN_TILE) == 0 and ATTN_SUB_TILES % 2 == 0
    assert w_gate.shape[2] % FF_TILE == 0 and seq % FFN_ROW_TILE == 0

    half = QK_ROPE_DIM // 2
    inv_freq = (ROPE_BASE ** (-jnp.arange(half, dtype=F32) / half)).reshape(half, 1)
    pos = positions.astype(F32).reshape(1, batch * seq)
    row = lambda v: v.reshape(1, -1)
    xs = x.reshape(batch * seq, d)

    wg, wu, wd = w_gate, w_up.astype(BF16), w_down
    zpad = jnp.zeros((d, LANE - QK_ROPE_DIM), F32)
    w_pe = w_kv_a[:, KV_LORA_RANK:]
    wa_ext = jnp.concatenate([w_kv_a[:, :KV_LORA_RANK], w_pe, zpad, _swap_halves(w_pe), zpad], axis=1).astype(BF16)
    wkb = w_kv_b.reshape(KV_LORA_RANK, N_HEADS, QK_NOPE_DIM + V_DIM)
    w_kn = wkb[:, :, :QK_NOPE_DIM].reshape(KV_LORA_RANK, -1).astype(BF16)
    w_vt = wkb[:, :, QK_NOPE_DIM:].reshape(KV_LORA_RANK, -1).T.astype(BF16)

    kn = vt = kpe = None
    for l in range(depth):
        if l < n_pool:
            xs = _pool_mixer(xs, seq, row(pool_norm[l]), pool_w[l].astype(BF16), row(pool_scale[l]))
        else:
            b = l - n_pool
            rank = w_q_b.shape[1]
            wqb = w_q_b[b].reshape(rank, N_HEADS, QK_DIM)
            wqb_ext = jnp.concatenate([
                wqb[:, :, :QK_NOPE_DIM].reshape(rank, -1),
                wqb[:, :, QK_NOPE_DIM:].reshape(rank, -1)], axis=1).T.astype(BF16)
            wqa = w_q_a[b].astype(BF16)
            if b == 0:
                kn, vt, kpe, qn, qp = _qkv_proj(xs, row(kv_in_norm), row(attn_norm[b]), wa_ext, row(kv_latent_norm),
                                                w_kn, w_vt, wqa, row(q_latent_norm[b]), wqb_ext, pos, inv_freq)
            else:
                qn, qp = _q_proj(xs, row(attn_norm[b]), wqa, row(q_latent_norm[b]), wqb_ext, pos, inv_freq)
            attn = _attention(qn, qp, kn, kpe, vt, batch, seq)
            xs = _out_proj(xs, attn, w_o[b].astype(BF16))
        last = l == depth - 1
        xs = _ffn(xs, row(ffn_norm[l]), wg, wu, wd, row(final_norm), layer=l, final_norm=last)
    return xs.reshape(batch, seq, d)
```

```python
import functools
import math

import jax
import jax.numpy as jnp
from jax import lax
from jax.experimental import pallas as pl
from jax.experimental.pallas import tpu as pltpu

N_HEADS = 16
QK_NOPE_DIM = 128
QK_ROPE_DIM = 64
QK_DIM = QK_NOPE_DIM + QK_ROPE_DIM
V_DIM = 128
KV_LORA_RANK = 512
POOL_WINDOWS = (2, 4, 8, 16)
ROPE_BASE = 10000.0
NORM_EPS = 1e-6

LANE = 128
SUBLANE = 8
assert POOL_WINDOWS == tuple(2 ** (k + 1) for k in range(len(POOL_WINDOWS)))
HALO = SUBLANE * len(POOL_WINDOWS)
SUM_ROWS = 16
Q_PRESCALE = (1.0 / math.sqrt(QK_DIM)) * math.log2(math.e)

ROW_TILE = 512
FF_TILE = 512
FFN_ROW_TILE = 1024
FFN_SUB_ROWS = 512
ATTN_TILE = 512
ATTN_SUB_TILES = 2
VMEM_LIMIT = 56 * 1024 * 1024
FFN_VMEM_LIMIT = 60 * 1024 * 1024

F32 = jnp.float32
BF16 = jnp.bfloat16


def _rms(x, g):
    ms = jnp.mean(x * x, axis=-1, keepdims=True)
    return x * lax.rsqrt(ms + NORM_EPS) * g


def _rope_cos_sin(pos_ref, invf_ref):
    ang = invf_ref[...] * pos_ref[...]
    return jnp.cos(ang), jnp.sin(ang)


def _params(semantics, vmem_limit=VMEM_LIMIT):
    return pltpu.CompilerParams(dimension_semantics=semantics, vmem_limit_bytes=vmem_limit)


def _pool_kernel(x_ref, halo_ref, g_ref, w_ref, sc_ref, o_ref, hh_ref, lv_ref, *, tiles_per_seq, tm, pg):
    t = pl.program_id(0) % tiles_per_seq
    g = g_ref[...]
    rows_end = HALO + tm
    hh_ref[HALO:, :] = _rms(x_ref[...], g)
    hh_ref[:HALO, :] = jnp.where(t == 0, 0.0, _rms(halo_ref[...], g))
    for k in range(1, len(POOL_WINDOWS) + 1):
        start, shift, cols = SUBLANE * k, 2 ** (k - 1), slice((k - 1) * pg, None)
        src = hh_ref if k == 1 else lv_ref.at[k % 2]
        lv_ref[(k - 1) % 2, start:rows_end, cols] = (src[start:rows_end, cols]
                                                     + src[start - shift:rows_end - shift, cols])
    pos = t * tm + lax.broadcasted_iota(jnp.int32, (tm, 1), 0)
    for gi, w in enumerate(POOL_WINDOWS):
        cols = slice(gi * pg, (gi + 1) * pg)
        count = jnp.minimum(pos + 1, w).astype(F32)
        diff = lv_ref[gi % 2, HALO:, cols] / count - hh_ref[HALO:, cols]
        y = jnp.dot(diff.astype(BF16), w_ref[gi], preferred_element_type=F32)
        o_ref[:, cols] = x_ref[:, cols] + y * sc_ref[:, cols]


def _pool_mixer(x, seq, norm, w, scale):
    tokens, d = x.shape
    tm = ROW_TILE
    pg = d // len(POOL_WINDOWS)
    kern = functools.partial(_pool_kernel, tiles_per_seq=seq // tm, tm=tm, pg=pg)
    return pl.pallas_call(
        kern,
        name="pool_mixer",
        grid=(tokens // tm,),
        in_specs=[
            pl.BlockSpec((tm, d), lambda i: (i, 0)),
            pl.BlockSpec((HALO, d), lambda i: (jnp.maximum(i * (tm // HALO) - 1, 0), 0)),
            pl.BlockSpec((1, d), lambda i: (0, 0)),
            pl.BlockSpec(w.shape, lambda i: (0, 0, 0)),
            pl.BlockSpec((1, d), lambda i: (0, 0)),
        ],
        out_specs=pl.BlockSpec((tm, d), lambda i: (i, 0)),
        out_shape=jax.ShapeDtypeStruct((tokens, d), F32),
        scratch_shapes=[pltpu.VMEM((HALO + tm, d), F32), pltpu.VMEM((2, HALO + tm, d), F32)],
        compiler_params=_params(("arbitrary",)),
    )(x, x, norm, w, scale)


def _ffn_kernel(x_ref, g_ref, wg_ref, wu_ref, wd_ref, fg_ref, o_ref, hn_ref, *, final_norm):
    j = pl.program_id(1)

    halves = [slice(r * FFN_SUB_ROWS, (r + 1) * FFN_SUB_ROWS) for r in range(x_ref.shape[0] // FFN_SUB_ROWS)]

    def chunk(first):
        for rows in halves:
            if first:
                hn = _rms(x_ref[rows, :], g_ref[...]).astype(BF16)
                hn_ref[rows, :] = hn
            else:
                hn = hn_ref[rows, :]
            gate = jnp.dot(hn, wg_ref[...].astype(BF16), preferred_element_type=F32)
            up = jnp.dot(hn, wu_ref[...], preferred_element_type=F32)
            act = gate / (1.0 + jnp.exp(-gate)) * up
            down = jnp.dot(act.astype(BF16), wd_ref[...].astype(BF16), preferred_element_type=F32)
            o_ref[rows, :] = (x_ref[rows, :] if first else o_ref[rows, :]) + down

    pl.when(j == 0)(functools.partial(chunk, True))
    pl.when(j > 0)(functools.partial(chunk, False))

    if final_norm:
        @pl.when(j == pl.num_programs(1) - 1)
        def _():
            for rows in halves:
                o_ref[rows, :] = _rms(o_ref[rows, :], fg_ref[...])


def _ffn(x, norm, w_gate, w_up, w_down, final_g, *, layer, final_norm):
    tokens, d = x.shape
    dff = w_gate.shape[2]
    tm, tf = FFN_ROW_TILE, FF_TILE
    kern = functools.partial(_ffn_kernel, final_norm=final_norm)
    return pl.pallas_call(
        kern,
        name="ffn_final" if final_norm else "ffn",
        grid=(tokens // tm, dff // tf),
        in_specs=[
            pl.BlockSpec((tm, d), lambda i, j: (i, 0)),
            pl.BlockSpec((1, d), lambda i, j: (0, 0)),
            pl.BlockSpec((None, d, tf), lambda i, j: (layer, 0, j)),
            pl.BlockSpec((None, d, tf), lambda i, j: (layer, 0, j)),
            pl.BlockSpec((None, tf, d), lambda i, j: (layer, j, 0)),
            pl.BlockSpec((1, d), lambda i, j: (0, 0)),
        ],
        out_specs=pl.BlockSpec((tm, d), lambda i, j: (i, 0)),
        out_shape=jax.ShapeDtypeStruct((tokens, d), F32),
        scratch_shapes=[pltpu.VMEM((tm, d), BF16)],
        compiler_params=_params(("arbitrary", "arbitrary"), FFN_VMEM_LIMIT),
    )(x, norm, w_gate, w_up, w_down, final_g)


def _kv_math(h, cos_t, sin_t, wa_ref, gl_ref, wkn_ref, wvt_ref, kn_ref, vt_ref, kpe_ref):
    kv = jnp.dot(h, wa_ref[...], preferred_element_type=F32)
    c_kv = _rms(kv[:, :KV_LORA_RANK], gl_ref[...]).astype(BF16)
    pe = kv[:, KV_LORA_RANK:KV_LORA_RANK + LANE]
    pe_swapped = kv[:, KV_LORA_RANK + LANE:]
    half = cos_t.shape[0]
    zeros = jnp.zeros((LANE - half, cos_t.shape[1]), F32)
    cos = jnp.concatenate([cos_t, zeros], axis=0).T
    sin = jnp.concatenate([sin_t, zeros], axis=0).T
    rot = (pe * cos - pe_swapped * sin) + pltpu.roll(pe_swapped * cos + pe * sin, half, axis=1)
    kpe_ref[:, :LANE] = rot.astype(BF16)
    kpe_ref[:, LANE:] = pltpu.roll(rot, LANE // 2, axis=1).astype(BF16)
    kn_ref[...] = jnp.dot(c_kv, wkn_ref[...], preferred_element_type=F32).astype(BF16)
    vt = lax.dot_general(wvt_ref[...], c_kv, (((1,), (1,)), ((), ())), preferred_element_type=F32)
    vt_ref[0] = vt.astype(BF16)


def _q_math(h, cos_t, sin_t, wa_ref, gl_ref, wb_ref, qn_ref, qp_ref):
    n_nope, n_pe = qn_ref.shape[0], qp_ref.shape[0]
    cq = jnp.dot(h, wa_ref[...], preferred_element_type=F32)
    cqn = _rms(cq, gl_ref[...]).astype(BF16)
    qt = lax.dot_general(wb_ref[...], cqn, (((1,), (1,)), ((), ())), preferred_element_type=F32)
    qn_ref[...] = (qt[:n_nope, :] * Q_PRESCALE).astype(BF16)
    cos_t, sin_t = cos_t * Q_PRESCALE, sin_t * Q_PRESCALE
    half = cos_t.shape[0]
    reps = LANE // (2 * half)
    cos = jnp.concatenate([cos_t, cos_t] * reps, axis=0)
    sin = jnp.concatenate([-sin_t, sin_t] * reps, axis=0)
    for p in range(n_pe // LANE):
        pe = qt[n_nope + p * LANE:n_nope + (p + 1) * LANE, :]
        sw = jnp.concatenate([pe[(2 * r + 1 - c) * half:(2 * r + 2 - c) * half, :]
                              for r in range(reps) for c in range(2)], axis=0)
        qp_ref[p * LANE:(p + 1) * LANE, :] = (pe * cos + sw * sin).astype(BF16)


def _qkv_kernel(x_ref, gkv_ref, gq_ref, wa_ref, gl_ref, wkn_ref, wvt_ref, wqa_ref, gql_ref, wqb_ref,
                pos_ref, invf_ref, kn_ref, vt_ref, kpe_ref, qn_ref, qp_ref):
    x = x_ref[...]
    xn = x * lax.rsqrt(jnp.mean(x * x, axis=-1, keepdims=True) + NORM_EPS)
    cos_t, sin_t = _rope_cos_sin(pos_ref, invf_ref)
    _kv_math((xn * gkv_ref[...]).astype(BF16), cos_t, sin_t, wa_ref, gl_ref, wkn_ref, wvt_ref, kn_ref, vt_ref, kpe_ref)
    _q_math((xn * gq_ref[...]).astype(BF16), cos_t, sin_t, wqa_ref, gql_ref, wqb_ref, qn_ref, qp_ref)


def _q_kernel(x_ref, g_ref, wa_ref, gl_ref, wb_ref, pos_ref, invf_ref, qn_ref, qp_ref):
    cos_t, sin_t = _rope_cos_sin(pos_ref, invf_ref)
    _q_math(_rms(x_ref[...], g_ref[...]).astype(BF16), cos_t, sin_t, wa_ref, gl_ref, wb_ref, qn_ref, qp_ref)


def _resident(shape):
    return pl.BlockSpec(shape, lambda i: (0,) * len(shape), pipeline_mode=pl.Buffered(1))


def _q_specs(tokens, tm, wb_ext):
    n_nope = N_HEADS * QK_NOPE_DIM
    n_pe = wb_ext.shape[0] - n_nope
    specs = [pl.BlockSpec((n_nope, tm), lambda i: (0, i)), pl.BlockSpec((n_pe, tm), lambda i: (0, i))]
    shapes = [jax.ShapeDtypeStruct((n_nope, tokens), BF16), jax.ShapeDtypeStruct((n_pe, tokens), BF16)]
    return specs, shapes


def _qkv_proj(x, kv_norm, q_norm, wa_ext, kv_latent_norm, w_kn, w_vt, wqa, q_latent_norm, wqb_ext, pos, inv_freq):
    tokens, d = x.shape
    tm = ATTN_TILE
    n_kn, n_v = w_kn.shape[1], w_vt.shape[0]
    q_specs, q_shapes = _q_specs(tokens, tm, wqb_ext)
    return pl.pallas_call(
        _qkv_kernel,
        name="qkv_proj",
        grid=(tokens // tm,),
        in_specs=[
            pl.BlockSpec((tm, d), lambda i: (i, 0)),
            _resident((1, d)), _resident((1, d)),
            _resident(wa_ext.shape), _resident((1, KV_LORA_RANK)), _resident(w_kn.shape), _resident(w_vt.shape),
            _resident(wqa.shape), _resident((1, wqa.shape[1])), _resident(wqb_ext.shape),
            pl.BlockSpec((1, tm), lambda i: (0, i)),
            _resident(inv_freq.shape),
        ],
        out_specs=[
            pl.BlockSpec((tm, n_kn), lambda i: (i, 0)),
            pl.BlockSpec((1, n_v, tm), lambda i: (i, 0, 0)),
            pl.BlockSpec((tm, 2 * LANE), lambda i: (i, 0)),
        ] + q_specs,
        out_shape=[
            jax.ShapeDtypeStruct((tokens, n_kn), BF16),
            jax.ShapeDtypeStruct((tokens // tm, n_v, tm), BF16),
            jax.ShapeDtypeStruct((tokens, 2 * LANE), BF16),
        ] + q_shapes,
        compiler_params=_params(("arbitrary",)),
    )(x, kv_norm, q_norm, wa_ext, kv_latent_norm, w_kn, w_vt, wqa, q_latent_norm, wqb_ext, pos, inv_freq)


def _q_proj(x, norm, wa, latent_norm, wb_ext, pos, inv_freq):
    tokens, d = x.shape
    tm = ROW_TILE
    q_specs, q_shapes = _q_specs(tokens, tm, wb_ext)
    return pl.pallas_call(
        _q_kernel,
        name="q_proj",
        grid=(tokens // tm,),
        in_specs=[
            pl.BlockSpec((tm, d), lambda i: (i, 0)),
            _resident((1, d)),
            _resident(wa.shape), _resident((1, wa.shape[1])), _resident(wb_ext.shape),
            pl.BlockSpec((1, tm), lambda i: (0, i)),
            _resident(inv_freq.shape),
        ],
        out_specs=q_specs,
        out_shape=q_shapes,
        compiler_params=_params(("arbitrary",)),
    )(x, norm, wa, latent_norm, wb_ext, pos, inv_freq)


def _attn_kernel(qn_ref, qp_ref, kn_ref, kp_ref, vt_ref, o_ref, st_ref, acc_ref, *, blk, n_sub):
    qi = pl.program_id(2)
    base = n_sub * qi
    qp = qp_ref[...]
    qts = [jnp.concatenate([qn_ref[h * LANE:(h + 1) * LANE, :], qp], axis=0) for h in range(2)]
    acc_ref[...] = jnp.zeros(acc_ref.shape, F32)

    def scores(h, j, first_sub):
        off = pl.multiple_of(j * blk, blk)
        k = jnp.concatenate([kn_ref[pl.ds(off, blk), h * LANE:(h + 1) * LANE],
                             kp_ref[pl.ds(off, blk), h * LANE:(h + 1) * LANE]], axis=1)
        return jnp.dot(k, qts[h][:, first_sub * blk:], preferred_element_type=F32)

    ones_rows = jnp.ones((SUM_ROWS, blk), BF16)

    def consume(h, s, slot, vblk, m, cmax):
        m_new = jnp.maximum(m, cmax)
        alpha = jnp.exp2(m - m_new)
        pt = jnp.exp2(st_ref[h, s, slot] - m_new)
        vt = jnp.concatenate([vt_ref[vblk, h * V_DIM:(h + 1) * V_DIM, :], ones_rows], axis=0)
        acc_ref[h, s] = alpha * acc_ref[h, s] + jnp.dot(vt, pt.astype(BF16), preferred_element_type=F32)
        return m_new

    def colmax(st):
        return jnp.max(st, axis=0, keepdims=True)

    def step(t, carry, parity, first_sub=0, mask=None):
        carry = list(carry)
        for h in range(2):
            st = scores(h, t, first_sub)
            for s in range(max(first_sub - 1, 0), n_sub):
                m, cmax = carry[n_sub * h + s]
                m = consume(h, s, 1 - parity, t - 1, m, cmax)
                if s >= first_sub:
                    st_s = st[:, (s - first_sub) * blk:(s - first_sub + 1) * blk]
                    if s == first_sub and mask is not None:
                        st_s = jnp.where(mask, st_s, -jnp.inf)
                    st_ref[h, s, parity] = st_s
                    cmax = colmax(st_s)
                carry[n_sub * h + s] = (m, cmax)
        return tuple(carry)

    key = lax.broadcasted_iota(jnp.int32, (blk, blk), 0)
    qry = lax.broadcasted_iota(jnp.int32, (blk, blk), 1)
    causal = key <= qry
    m0 = jnp.full((1, blk), -jnp.inf, F32)

    first_mask = jnp.logical_or(causal, qi > 0)
    carry = []
    for h in range(2):
        st = scores(h, 0, 0)
        for s in range(n_sub):
            st_s = st[:, s * blk:(s + 1) * blk]
            if s == 0:
                st_s = jnp.where(first_mask, st_s, -jnp.inf)
            st_ref[h, s, 0] = st_s
            carry.append((m0, colmax(st_s)))

    def pair(u, carry):
        return step(2 * u + 2, step(2 * u + 1, carry, 1), 0)

    n_pairs = jnp.maximum(base // 2 - 1, 0)
    carry = lax.fori_loop(0, n_pairs // 2, lambda u, c: pair(2 * u + 1, pair(2 * u, c)), tuple(carry))
    carry = lax.cond(n_pairs % 2 == 1, lambda c: pair(n_pairs - 1, c), lambda c: c, carry)
    carry = lax.cond(qi >= 1,
                     lambda c: step(base, step(base - 1, c, 1), 0, 0, causal),
                     lambda c: c, carry)

    for k in range(1, n_sub):
        carry = step(base + k, carry, k % 2, k, causal)

    last = n_sub - 1
    for h in range(2):
        consume(h, last, last % 2, base + last, *carry[n_sub * h + last])
        for s in range(n_sub):
            out = acc_ref[h, s, :V_DIM, :] / acc_ref[h, s, V_DIM:V_DIM + 1, :]
            o_ref[s * blk:(s + 1) * blk, h * LANE:(h + 1) * LANE] = out.T.astype(BF16)


def _attention(qn, qp, kn, kpe, vt, batch, seq):
    tokens = qn.shape[1]
    blk, n_sub = ATTN_TILE, ATTN_SUB_TILES
    rows = n_sub * blk
    nk = seq // blk
    nq = seq // rows
    kern = functools.partial(_attn_kernel, blk=blk, n_sub=n_sub)
    return pl.pallas_call(
        kern,
        name="attention",
        grid=(batch, N_HEADS // 2, nq),
        in_specs=[
            pl.BlockSpec((2 * LANE, rows), lambda b, p, i: (p, b * nq + i)),
            pl.BlockSpec((LANE, rows), lambda b, p, i: (p, b * nq + i)),
            pl.BlockSpec((seq, 2 * LANE), lambda b, p, i: (b, p)),
            pl.BlockSpec((seq, 2 * LANE), lambda b, p, i: (b, 0)),
            pl.BlockSpec((nk, 2 * V_DIM, blk), lambda b, p, i: (b, p, 0)),
        ],
        out_specs=pl.BlockSpec((rows, 2 * LANE), lambda b, p, i: (b * nq + i, p)),
        out_shape=jax.ShapeDtypeStruct((tokens, N_HEADS * V_DIM), BF16),
        scratch_shapes=[pltpu.VMEM((2, n_sub, 2, blk, blk), F32),
                        pltpu.VMEM((2, n_sub, V_DIM + SUM_ROWS, blk), F32)],
        compiler_params=_params(("arbitrary", "arbitrary", "arbitrary")),
    )(qn, qp, kn, kpe, vt)


def _out_proj_kernel(x_ref, a_ref, w_ref, o_ref):
    o_ref[...] = x_ref[...] + jnp.dot(a_ref[...], w_ref[...], preferred_element_type=F32)


def _out_proj(x, attn, w_o):
    tokens, d = x.shape
    tm = ROW_TILE
    return pl.pallas_call(
        _out_proj_kernel,
        name="out_proj",
        grid=(tokens // tm,),
        in_specs=[
            pl.BlockSpec((tm, d), lambda i: (i, 0)),
            pl.BlockSpec((tm, attn.shape[1]), lambda i: (i, 0)),
            pl.BlockSpec(w_o.shape, lambda i: (0, 0)),
        ],
        out_specs=pl.BlockSpec((tm, d), lambda i: (i, 0)),
        out_shape=jax.ShapeDtypeStruct((tokens, d), F32),
        compiler_params=_params(("arbitrary",)),
    )(x, attn, w_o)


def _swap_halves(w):
    half = w.shape[-1] // 2
    return jnp.concatenate([w[..., half:], w[..., :half]], axis=-1)


def kernel(x, positions, pool_norm, pool_w, pool_scale, kv_in_norm, w_kv_a, kv_latent_norm, w_kv_b, attn_norm, w_q_a, q_latent_norm, w_q_b, w_o, ffn_norm, w_gate, w_up, w_down, final_norm):
    batch, seq, d = x.shape
    depth = ffn_norm.shape[0]
    n_pool = pool_norm.shape[0]
    assert seq % ROW_TILE == 0 and seq % (ATTN_SUB_TILES * ATTN_TILE) == 0 and ATTN_SUB_TILES % 2 == 0
    assert w_gate.shape[2] % FF_TILE == 0 and seq % FFN_ROW_TILE == 0

    half = QK_ROPE_DIM // 2
    inv_freq = (ROPE_BASE ** (-jnp.arange(half, dtype=F32) / half)).reshape(half, 1)
    pos = positions.astype(F32).reshape(1, batch * seq)
    row = lambda v: v.reshape(1, -1)
    xs = x.reshape(batch * seq, d)

    wg, wu, wd = w_gate, w_up.astype(BF16), w_down
    zpad = jnp.zeros((d, LANE - QK_ROPE_DIM), F32)
    w_pe = w_kv_a[:, KV_LORA_RANK:]
    wa_ext = jnp.concatenate([w_kv_a[:, :KV_LORA_RANK], w_pe, zpad, _swap_halves(w_pe), zpad], axis=1).astype(BF16)
    wkb = w_kv_b.reshape(KV_LORA_RANK, N_HEADS, QK_NOPE_DIM + V_DIM)
    w_kn = wkb[:, :, :QK_NOPE_DIM].reshape(KV_LORA_RANK, -1).astype(BF16)
    w_vt = wkb[:, :, QK_NOPE_DIM:].reshape(KV_LORA_RANK, -1).T.astype(BF16)

    kn = vt = kpe = None
    for l in range(depth):
        if l < n_pool:
            xs = _pool_mixer(xs, seq, row(pool_norm[l]), pool_w[l].astype(BF16), row(pool_scale[l]))
        else:
            b = l - n_pool
            rank = w_q_b.shape[1]
            wqb = w_q_b[b].reshape(rank, N_HEADS, QK_DIM)
            wqb_ext = jnp.concatenate([
                wqb[:, :, :QK_NOPE_DIM].reshape(rank, -1),
                wqb[:, :, QK_NOPE_DIM:].reshape(rank, -1)], axis=1).T.astype(BF16)
            wqa = w_q_a[b].astype(BF16)
            if b == 0:
                kn, vt, kpe, qn, qp = _qkv_proj(xs, row(kv_in_norm), row(attn_norm[b]), wa_ext, row(kv_latent_norm),
                                                w_kn, w_vt, wqa, row(q_latent_norm[b]), wqb_ext, pos, inv_freq)
            else:
                qn, qp = _q_proj(xs, row(attn_norm[b]), wqa, row(q_latent_norm[b]), wqb_ext, pos, inv_freq)
            attn = _attention(qn, qp, kn, kpe, vt, batch, seq)
            xs = _out_proj(xs, attn, w_o[b].astype(BF16))
        last = l == depth - 1
        xs = _ffn(xs, row(ffn_norm[l]), wg, wu, wd, row(final_norm), layer=l, final_norm=last)
    return xs.reshape(batch, seq, d)
```
